```python
import math
import numpy as np
import jax
import jax.numpy as jnp
from jax import lax

D_MODEL = 1024
BATCH = 8
SEQ = 2048
DEPTH = 2

CTX_LEN = 256
GRID_W = 64
Q_BLOCK = 128
ROPE_BASE = 10000.0
NEG_INF = -1e30
EPS = 1e-6
N_ADA = 6

MLA_HEADS = 4
MLA_NOPE = 64
MLA_ROPE = 32
MLA_V = 64
MLA_Q_RANK = 256
MLA_KV_RANK = 128
MLA_SCALE = (MLA_NOPE + MLA_ROPE) ** -0.5
DIFF_HEADS = 4
DIFF_QK = 32
DIFF_V = 2 * DIFF_QK
DIFF_SCALE = DIFF_QK ** -0.5
WIN_HEADS = 4
WIN_KV_HEADS = 2
WIN_GROUP = WIN_HEADS // WIN_KV_HEADS
WIN_DIM = 64
WINDOW = 128
WIN_SCALE = WIN_DIM ** -0.5
NA_HEADS = 4
NA_DIM = 64
NA_ROWS = 8
NA_COLS = 16
NA_SCALE = NA_DIM ** -0.5
FFN_HIDDEN = 4 * D_MODEL

N_BRANCH = 4
BRANCH_WIDTHS = (MLA_HEADS * MLA_V, DIFF_HEADS * DIFF_V, WIN_HEADS * WIN_DIM, NA_HEADS * NA_DIM)
MIX_WIDTH = sum(BRANCH_WIDTHS)
IN_SPLITS = (MLA_Q_RANK, MLA_KV_RANK, MLA_ROPE,
             DIFF_HEADS * 2 * DIFF_QK, DIFF_HEADS * 2 * DIFF_QK, DIFF_HEADS * DIFF_V,
             WIN_HEADS * WIN_DIM, WIN_KV_HEADS * WIN_DIM, WIN_KV_HEADS * WIN_DIM,
             NA_HEADS * NA_DIM, NA_HEADS * NA_DIM, NA_HEADS * NA_DIM)
IN_WIDTH = sum(IN_SPLITS)

kernel_name = 'hybrid_gated_mixer_dit_block'


def rmsnorm(x, g):
    xf = x.astype(jnp.float32)
    y = xf * lax.rsqrt(jnp.mean(xf * xf, axis=-1, keepdims=True) + EPS)
    return (y * g.astype(jnp.float32)).astype(x.dtype)


def adaln(x, g, shift, scale):
    return rmsnorm(x, g) * (1 + scale) + shift


def split_columns(p):
    offsets = np.cumsum(IN_SPLITS)[:-1].tolist()
    return jnp.split(p, offsets, axis=-1)


def axial_rope_tables(n_tokens, dim):
    quarter = dim // 4
    inv_freq = ROPE_BASE ** (-jnp.arange(quarter, dtype=jnp.float32) / quarter)
    t = jnp.arange(n_tokens, dtype=jnp.int32)
    row = (t // GRID_W).astype(jnp.float32)[:, None] * inv_freq
    col = (t % GRID_W).astype(jnp.float32)[:, None] * inv_freq
    ang = jnp.concatenate([row, row, col, col], axis=-1)
    return jnp.cos(ang), jnp.sin(ang)


def rope2d(x, rope):
    cos, sin = rope
    x1, x2, x3, x4 = jnp.split(x, 4, axis=-1)
    rot = jnp.concatenate([-x2, x1, -x4, x3], axis=-1)
    return x * cos[:, None].astype(x.dtype) + rot * sin[:, None].astype(x.dtype)


def sweep_query_blocks(fn, *qs):
    B, n = qs[0].shape[:2]
    nb = n // Q_BLOCK
    blocks = tuple(jnp.moveaxis(q.reshape(B, nb, Q_BLOCK, *q.shape[2:]), 1, 0) for q in qs)
    out = lax.map(lambda args: fn(*args), blocks)
    return jnp.moveaxis(out, 0, 1).reshape(B, n, *out.shape[3:])


def dense_attention(q, k, v, scale):
    s = jnp.einsum('bqhd,bkhd->bhqk', q, k).astype(jnp.float32) * scale
    p = jax.nn.softmax(s, axis=-1).astype(v.dtype)
    return jnp.einsum('bhqk,bkhd->bqhd', p, v)


def mla_branch(lat, ctx, q_norm, kv_norm, w_uq, w_ukv, rope, need_ctx):
    cq_l, ckv_l, kr_l = lat
    cq_c, ckv_c, kr_c = ctx
    B, n = cq_l.shape[:2]

    def up_q(c_q):
        q = (rmsnorm(c_q, q_norm) @ w_uq).reshape(c_q.shape[0], c_q.shape[1], MLA_HEADS, MLA_NOPE + MLA_ROPE)
        return q[..., :MLA_NOPE], q[..., MLA_NOPE:]

    def up_kv(c_kv):
        kv = (rmsnorm(c_kv, kv_norm) @ w_ukv).reshape(c_kv.shape[0], c_kv.shape[1], MLA_HEADS, MLA_NOPE + MLA_V)
        return kv[..., :MLA_NOPE], kv[..., MLA_NOPE:]

    def attend(qn, qr, kn, kr, v):
        s = (jnp.einsum('bqhd,bkhd->bhqk', qn, kn)
             + jnp.einsum('bqhr,bkr->bhqk', qr, kr)).astype(jnp.float32) * MLA_SCALE
        p = jax.nn.softmax(s, axis=-1).astype(v.dtype)
        return jnp.einsum('bhqk,bkhd->bqhd', p, v)

    qn_l, qr_l = up_q(cq_l)
    qr_l = rope2d(qr_l, rope)
    kn_l, v_l = up_kv(ckv_l)
    kr_l = rope2d(kr_l[:, :, None, :], rope)[:, :, 0, :]
    kn_c, v_c = up_kv(ckv_c)
    kn_all = jnp.concatenate([kn_c, kn_l], axis=1)
    kr_all = jnp.concatenate([kr_c, kr_l], axis=1)
    v_all = jnp.concatenate([v_c, v_l], axis=1)
    o_lat = sweep_query_blocks(lambda a, b: attend(a, b, kn_all, kr_all, v_all), qn_l, qr_l)
    o_lat = o_lat.reshape(B, n, MLA_HEADS * MLA_V)
    o_ctx = None
    if need_ctx:
        qn_c, qr_c = up_q(cq_c)
        o_ctx = attend(qn_c, qr_c, kn_c, kr_c, v_c).reshape(B, -1, MLA_HEADS * MLA_V)
    return o_lat, o_ctx


def diff_branch(lat, ctx, lam_params, subln, lam_init, rope, need_ctx):
    q_l, k_l, v_l = lat
    q_c, k_c, v_c = ctx
    B, n = q_l.shape[:2]

    def split_heads(t):
        t = t.reshape(t.shape[0], t.shape[1], DIFF_HEADS, 2, DIFF_QK)
        return t[..., 0, :], t[..., 1, :]

    def vheads(t):
        return t.reshape(t.shape[0], t.shape[1], DIFF_HEADS, DIFF_V)

    lp = lam_params.astype(jnp.float32)
    lam = jnp.exp(jnp.sum(lp[0] * lp[1])) - jnp.exp(jnp.sum(lp[2] * lp[3])) + lam_init

    def attend(q1, q2, k1, k2, v):
        s1 = jnp.einsum('bqhd,bkhd->bhqk', q1, k1).astype(jnp.float32) * DIFF_SCALE
        s2 = jnp.einsum('bqhd,bkhd->bhqk', q2, k2).astype(jnp.float32) * DIFF_SCALE
        p = jax.nn.softmax(s1, axis=-1) - lam * jax.nn.softmax(s2, axis=-1)
        return jnp.einsum('bhqk,bkhd->bqhd', p.astype(v.dtype), v)

    def finish(o):
        return (rmsnorm(o, subln) * (1 - lam_init)).reshape(o.shape[0], o.shape[1], DIFF_HEADS * DIFF_V)

    q1_l, q2_l = split_heads(q_l)
    k1_l, k2_l = split_heads(k_l)
    q1_l, q2_l, k1_l, k2_l = [rope2d(t, rope) for t in (q1_l, q2_l, k1_l, k2_l)]
    k1_c, k2_c = split_heads(k_c)
    vc = vheads(v_c)
    k1_all = jnp.concatenate([k1_c, k1_l], axis=1)
    k2_all = jnp.concatenate([k2_c, k2_l], axis=1)
    v_all = jnp.concatenate([vc, vheads(v_l)], axis=1)
    o_lat = finish(sweep_query_blocks(lambda a1, a2: attend(a1, a2, k1_all, k2_all, v_all), q1_l, q2_l))
    o_ctx = None
    if need_ctx:
        q1_c, q2_c = split_heads(q_c)
        o_ctx = finish(attend(q1_c, q2_c, k1_c, k2_c, vc))
    return o_lat, o_ctx


def window_branch(lat, ctx, sink, rope, need_ctx):
    q_l, k_l, v_l = lat
    q_c, k_c, v_c = ctx
    B, n = q_l.shape[:2]
    nb = n // WINDOW
    q = rope2d(q_l.reshape(B, n, WIN_HEADS, WIN_DIM), rope)
    k = rope2d(k_l.reshape(B, n, WIN_KV_HEADS, WIN_DIM), rope)
    v = v_l.reshape(B, n, WIN_KV_HEADS, WIN_DIM)
    kc = k_c.reshape(B, -1, WIN_KV_HEADS, WIN_DIM)
    vc = v_c.reshape(B, -1, WIN_KV_HEADS, WIN_DIM)
    sink_g = sink.astype(jnp.float32).reshape(WIN_KV_HEADS, WIN_GROUP)

    def banded(t):
        tp = jnp.pad(t, ((0, 0), (WINDOW, WINDOW), (0, 0), (0, 0))).reshape(B, nb + 2, WINDOW, WIN_KV_HEADS, WIN_DIM)
        return jnp.concatenate([tp[:, :-2], tp[:, 1:-1], tp[:, 2:]], axis=2)

    kb, vb = banded(k), banded(v)
    qg = q.reshape(B, nb, WINDOW, WIN_KV_HEADS, WIN_GROUP, WIN_DIM)
    blk = np.arange(nb)[:, None, None]
    qpos = blk * WINDOW + np.arange(WINDOW)[None, :, None]
    kpos = blk * WINDOW - WINDOW + np.arange(3 * WINDOW)[None, None, :]
    band = (kpos >= 0) & (kpos < n) & (np.abs(qpos - kpos) <= WINDOW)
    s_loc = jnp.einsum('bnqkgd,bnjkd->bnkgqj', qg, kb).astype(jnp.float32) * WIN_SCALE
    s_loc = jnp.where(band[None, :, None, None], s_loc, NEG_INF)
    s_ctx = jnp.einsum('bnqkgd,bjkd->bnkgqj', qg, kc).astype(jnp.float32) * WIN_SCALE
    s_sink = jnp.broadcast_to(sink_g[None, None, :, :, None, None], s_loc.shape[:-1] + (1,))
    p = jax.nn.softmax(jnp.concatenate([s_loc, s_ctx, s_sink], axis=-1), axis=-1).astype(v.dtype)
    n_loc = 3 * WINDOW
    n_ctx = kc.shape[1]
    o = (jnp.einsum('bnkgqj,bnjkd->bnqkgd', p[..., :n_loc], vb)
         + jnp.einsum('bnkgqj,bjkd->bnqkgd', p[..., n_loc:n_loc + n_ctx], vc))
    o_lat = o.reshape(B, n, WIN_HEADS * WIN_DIM)
    o_ctx = None
    if need_ctx:
        qcg = q_c.reshape(B, -1, WIN_KV_HEADS, WIN_GROUP, WIN_DIM)
        s = jnp.einsum('bqkgd,bjkd->bkgqj', qcg, kc).astype(jnp.float32) * WIN_SCALE
        s_sink_c = jnp.broadcast_to(sink_g[None, :, :, None, None], s.shape[:-1] + (1,))
        pc = jax.nn.softmax(jnp.concatenate([s, s_sink_c], axis=-1), axis=-1)[..., :-1].astype(v.dtype)
        o_ctx = jnp.einsum('bkgqj,bjkd->bqkgd', pc, vc).reshape(B, -1, WIN_HEADS * WIN_DIM)
    return o_lat, o_ctx


def neighborhood_branch(lat, ctx, rpb, need_ctx):
    q_l, k_l, v_l = lat
    q_c, k_c, v_c = ctx
    B, n = q_l.shape[:2]
    rows = n // GRID_W
    wr = min(NA_ROWS, rows)
    ncb = GRID_W // NA_COLS
    n_kcol = 2 * NA_COLS
    r = np.arange(rows)
    row_idx = np.clip(r - wr // 2, 0, rows - wr)[:, None] + np.arange(wr)[None, :]
    col0 = np.clip(np.arange(ncb) * NA_COLS - NA_COLS // 2, 0, GRID_W - n_kcol)
    col_idx = col0[:, None] + np.arange(n_kcol)[None, :]
    qcol = np.arange(ncb)[:, None] * NA_COLS + np.arange(NA_COLS)[None, :]
    cstart = np.clip(qcol - NA_COLS // 2, 0, GRID_W - NA_COLS)
    kcol = col_idx[:, None, :]
    in_win = (kcol >= cstart[..., None]) & (kcol < cstart[..., None] + NA_COLS)
    d_row = row_idx - r[:, None] + (NA_ROWS - 1)
    d_col = np.clip(kcol - qcol[..., None], -(NA_COLS - 1), NA_COLS - 1) + (NA_COLS - 1)
    bias = rpb.astype(jnp.float32)[:, d_row[:, None, None, :, None], d_col[None, :, :, None, :]]
    bias = jnp.where(in_win[None, None, :, :, None, :], bias, NEG_INF)
    bias = jnp.transpose(bias, (1, 2, 0, 3, 4, 5)).reshape(rows, ncb, NA_HEADS, NA_COLS, wr * n_kcol)

    q = q_l.reshape(B, rows, ncb, NA_COLS, NA_HEADS, NA_DIM)
    kgrid = k_l.reshape(B, rows, GRID_W, NA_HEADS, NA_DIM)
    vgrid = v_l.reshape(B, rows, GRID_W, NA_HEADS, NA_DIM)
    gr = row_idx[:, None, :, None]
    gc = col_idx[None, :, None, :]
    kg = kgrid[:, gr, gc]
    vg = vgrid[:, gr, gc]
    kcx = k_c.reshape(B, -1, NA_HEADS, NA_DIM)
    vcx = v_c.reshape(B, -1, NA_HEADS, NA_DIM)
    s_loc = jnp.einsum('brcqhd,brcwjhd->brchqwj', q, kg).astype(jnp.float32)
    s_loc = s_loc.reshape(B, rows, ncb, NA_HEADS, NA_COLS, wr * n_kcol) * NA_SCALE + bias[None]
    s_ctx = jnp.einsum('brcqhd,bjhd->brchqj', q, kcx).astype(jnp.float32) * NA_SCALE
    p = jax.nn.softmax(jnp.concatenate([s_loc, s_ctx], axis=-1), axis=-1).astype(v_l.dtype)
    n_loc = wr * n_kcol
    p_loc = p[..., :n_loc].reshape(B, rows, ncb, NA_HEADS, NA_COLS, wr, n_kcol)
    o = (jnp.einsum('brchqwj,brcwjhd->brcqhd', p_loc, vg)
         + jnp.einsum('brchqj,bjhd->brcqhd', p[..., n_loc:], vcx))
    o_lat = o.reshape(B, n, NA_HEADS * NA_DIM)
    o_ctx = None
    if need_ctx:
        qcx = q_c.reshape(B, -1, NA_HEADS, NA_DIM)
        o_ctx = dense_attention(qcx, kcx, vcx, NA_SCALE).reshape(B, -1, NA_HEADS * NA_DIM)
    return o_lat, o_ctx


def merge_branches(h, outs, w_gate, b_gate, w_branch):
    y = None
    off = 0
    for i, (o, width) in enumerate(zip(outs, BRANCH_WIDTHS)):
        g = jax.nn.sigmoid(h @ w_gate[:, i * D_MODEL:(i + 1) * D_MODEL] + b_gate[i * D_MODEL:(i + 1) * D_MODEL])
        term = g * (o @ w_branch[off:off + width])
        y = term if y is None else y + term
        off += width
    return y


def squared_relu_mlp(h, w1, w2):
    return jnp.square(jax.nn.relu(h @ w1)) @ w2


def lambda_init(layer):
    return 0.8 - 0.6 * math.exp(-0.3 * layer)


def setup_inputs(seed: int = 0) -> dict:
    key = jax.random.key(seed)
    ks = jax.random.split(key, 24)
    D = D_MODEL

    def nrm(k, shape, scale):
        return scale * jax.random.normal(k, shape, jnp.float32)

    return {
        'x': nrm(ks[0], (BATCH, SEQ, D), 1.0),
        'c': nrm(ks[1], (BATCH, D), 1.0),
        'ctx': nrm(ks[2], (BATCH, CTX_LEN, D), 1.0),
        'c_ctx': nrm(ks[3], (D,), 1.0),
        'w_ada': nrm(ks[4], (DEPTH, D, N_ADA * D), 0.5 * D ** -0.5),
        'b_ada': nrm(ks[5], (DEPTH, N_ADA * D), 0.02),
        'norm_mix': 1.0 + nrm(ks[6], (DEPTH, D), 0.02),
        'norm_ffn': 1.0 + nrm(ks[7], (DEPTH, D), 0.02),
        'w_in': nrm(ks[8], (DEPTH, D, IN_WIDTH), D ** -0.5),
        'mla_q_norm': 1.0 + nrm(ks[9], (DEPTH, MLA_Q_RANK), 0.02),
        'mla_kv_norm': 1.0 + nrm(ks[10], (DEPTH, MLA_KV_RANK), 0.02),
        'w_uq': nrm(ks[11], (DEPTH, MLA_Q_RANK, MLA_HEADS * (MLA_NOPE + MLA_ROPE)), MLA_Q_RANK ** -0.5),
        'w_ukv': nrm(ks[12], (DEPTH, MLA_KV_RANK, MLA_HEADS * (MLA_NOPE + MLA_V)), MLA_KV_RANK ** -0.5),
        'diff_lambda': nrm(ks[13], (DEPTH, 4, DIFF_QK), 0.1),
        'diff_subln': 1.0 + nrm(ks[14], (DEPTH, DIFF_V), 0.02),
        'win_sink': nrm(ks[15], (DEPTH, WIN_HEADS), 0.5),
        'na_rpb': nrm(ks[16], (DEPTH, NA_HEADS, 2 * NA_ROWS - 1, 2 * NA_COLS - 1), 0.1),
        'w_gate': nrm(ks[17], (DEPTH, D, N_BRANCH * D), D ** -0.5),
        'b_gate': nrm(ks[18], (DEPTH, N_BRANCH * D), 0.02),
        'w_branch': nrm(ks[19], (DEPTH, MIX_WIDTH, D), (MIX_WIDTH // N_BRANCH) ** -0.5),
        'w_out': nrm(ks[20], (DEPTH, D, D), D ** -0.5),
        'w_ff1': nrm(ks[21], (DEPTH, D, FFN_HIDDEN), D ** -0.5),
        'w_ff2': nrm(ks[22], (DEPTH, FFN_HIDDEN, D), FFN_HIDDEN ** -0.5),
        'final_norm': 1.0 + nrm(ks[23], (D,), 0.02),
    }


def reference(x, c, ctx, c_ctx, w_ada, b_ada, norm_mix, norm_ffn, w_in, mla_q_norm, mla_kv_norm,
              w_uq, w_ukv, diff_lambda, diff_subln, win_sink, na_rpb, w_gate, b_gate, w_branch,
              w_out, w_ff1, w_ff2, final_norm):
    n_lat = x.shape[1]
    rope_mla = axial_rope_tables(n_lat, MLA_ROPE)
    rope_diff = axial_rope_tables(n_lat, DIFF_QK)
    rope_win = axial_rope_tables(n_lat, WIN_DIM)
    xc = ctx
    for l in range(DEPTH):
        need_ctx = l < DEPTH - 1
        mod_lat = jnp.split((jax.nn.silu(c) @ w_ada[l] + b_ada[l])[:, None, :], N_ADA, axis=-1)
        mod_ctx = jnp.split(jax.nn.silu(c_ctx) @ w_ada[l] + b_ada[l], N_ADA, axis=-1)
        h_lat = adaln(x, norm_mix[l], mod_lat[0], mod_lat[1])
        h_ctx = adaln(xc, norm_mix[l], mod_ctx[0], mod_ctx[1])
        p_lat = split_columns(h_lat @ w_in[l])
        p_ctx = split_columns(h_ctx @ w_in[l])
        oa_l, oa_c = mla_branch(p_lat[0:3], p_ctx[0:3], mla_q_norm[l], mla_kv_norm[l], w_uq[l], w_ukv[l], rope_mla, need_ctx)
        ob_l, ob_c = diff_branch(p_lat[3:6], p_ctx[3:6], diff_lambda[l], diff_subln[l], lambda_init(l), rope_diff, need_ctx)
        oc_l, oc_c = window_branch(p_lat[6:9], p_ctx[6:9], win_sink[l], rope_win, need_ctx)
        od_l, od_c = neighborhood_branch(p_lat[9:12], p_ctx[9:12], na_rpb[l], need_ctx)
        mix_lat = merge_branches(h_lat, (oa_l, ob_l, oc_l, od_l), w_gate[l], b_gate[l], w_branch[l])
        x = x + mod_lat[2] * (mix_lat @ w_out[l])
        h = adaln(x, norm_ffn[l], mod_lat[3], mod_lat[4])
        x = x + mod_lat[5] * squared_relu_mlp(h, w_ff1[l], w_ff2[l])
        if need_ctx:
            mix_ctx = merge_branches(h_ctx, (oa_c, ob_c, oc_c, od_c), w_gate[l], b_gate[l], w_branch[l])
            xc = xc + mod_ctx[2] * (mix_ctx @ w_out[l])
            hc = adaln(xc, norm_ffn[l], mod_ctx[3], mod_ctx[4])
            xc = xc + mod_ctx[5] * squared_relu_mlp(hc, w_ff1[l], w_ff2[l])
    return rmsnorm(x, final_norm)
```

```python
import functools
import math

import numpy as np
import jax
import jax.numpy as jnp
from jax import lax
from jax.experimental import pallas as pl
from jax.experimental.pallas import tpu as pltpu

D_MODEL = 1024
BATCH = 8
SEQ = 2048
DEPTH = 2
CTX_LEN = 256
T_ALL = SEQ + CTX_LEN
GRID_W = 64
GRID_ROWS = SEQ // GRID_W
ROPE_BASE = 10000.0
NEG_INF = -1e30
EPS = 1e-6
N_ADA = 6

MLA_HEADS = 4
MLA_NOPE = 64
MLA_ROPE = 32
MLA_V = 64
MLA_Q_RANK = 256
MLA_KV_RANK = 128
MLA_SCALE = (MLA_NOPE + MLA_ROPE) ** -0.5
DIFF_HEADS = 4
DIFF_QK = 32
DIFF_V = 64
DIFF_SCALE = DIFF_QK ** -0.5
WIN_HEADS = 4
WIN_KV_HEADS = 2
WIN_DIM = 64
WINDOW = 128
WIN_SCALE = WIN_DIM ** -0.5
NA_HEADS = 4
NA_DIM = 64
NA_ROWS = 8
NA_COLS = 16
NA_SCALE = NA_DIM ** -0.5
FFN_HIDDEN = 4 * D_MODEL
N_BRANCH = 4
BRANCH_W = 256

LANES = 128
HALF = LANES // 2
VMEM_LIMIT = 56 * 1024 * 1024

BF16 = jnp.bfloat16
F32 = jnp.float32

P_CQ, P_CKV, P_KR = 0, 256, 384
P_DQ, P_DK, P_DV = 512, 768, 1024
P_WQ, P_WK, P_WV = 1280, 1536, 1664
P_NQ, P_NK, P_NV = 1792, 2048, 2304
P_WIDTH = 2560

TM_PROJ = 256
TQ_DENSE = 256
TQ_WIN = WINDOW
TQ_NA = GRID_W
TM_MLP = 256
N_MOD_ROWS = 16
CTX_MOD_ROW = BATCH


def _const_spec(shape):
    nd = len(shape)
    return pl.BlockSpec(shape, lambda *_: (0,) * nd, pipeline_mode=pl.Buffered(1))


def _params(n_axes):
    return pltpu.CompilerParams(dimension_semantics=("arbitrary",) * n_axes, vmem_limit_bytes=VMEM_LIMIT)


def _rms(x, g):
    return x * lax.rsqrt(jnp.mean(x * x, axis=-1, keepdims=True) + EPS) * g


def _adaln(x, g, shift, scale):
    return _rms(x, g) * (1.0 + scale) + shift


def _dot(a, b):
    return jnp.dot(a, b, preferred_element_type=F32)


def _dot_nt(a, b):
    return lax.dot_general(a, b, (((1,), (1,)), ((), ())), preferred_element_type=F32)


def _lane_iota(rows):
    return lax.broadcasted_iota(jnp.int32, (rows, LANES), 1)


def _softmax_parts(parts, extra=None):
    m = parts[0].max(axis=-1, keepdims=True)
    for s in parts[1:]:
        m = jnp.maximum(m, s.max(axis=-1, keepdims=True))
    if extra is not None:
        m = jnp.maximum(m, extra)
    es = [jnp.exp(s - m) for s in parts]
    l = es[0].sum(axis=-1, keepdims=True)
    for e in es[1:]:
        l = l + e.sum(axis=-1, keepdims=True)
    if extra is not None:
        l = l + jnp.exp(extra - m)
    r = 1.0 / l
    return [e * r for e in es]


def _mod_kernel(c_ref, w_ref, b_ref, o_ref):
    c = c_ref[...]
    act = c * jax.nn.sigmoid(c)
    o_ref[0] = jnp.dot(act, w_ref[0], precision=lax.Precision.HIGHEST, preferred_element_type=F32) + b_ref[0]


def _modulation(cvec, w_ada, b_ada):
    tn = 1536
    n_ada = N_ADA * D_MODEL
    out = pl.pallas_call(
        _mod_kernel,
        grid=(DEPTH, n_ada // tn),
        in_specs=[
            pl.BlockSpec((N_MOD_ROWS, D_MODEL), lambda l, n: (0, 0)),
            pl.BlockSpec((1, D_MODEL, tn), lambda l, n: (l, 0, n)),
            pl.BlockSpec((1, 1, tn), lambda l, n: (l, 0, n)),
        ],
        out_specs=pl.BlockSpec((1, N_MOD_ROWS, tn), lambda l, n: (l, 0, n)),
        out_shape=jax.ShapeDtypeStruct((DEPTH, N_MOD_ROWS, n_ada), F32),
        compiler_params=_params(2),
        name="modulation",
    )(cvec, w_ada, b_ada.reshape(DEPTH, 1, n_ada))
    return out.reshape(DEPTH, N_MOD_ROWS, N_ADA, D_MODEL)


def _mod_index(n_lat_tiles):
    return lambda j, b: (jnp.where(j >= n_lat_tiles, CTX_MOD_ROW, b), 0, 0)


def _rope(x, tab_ref, base, quarter):
    return (x * tab_ref[base]
            + pltpu.roll(x, LANES - quarter, 1) * tab_ref[base + 1]
            + pltpu.roll(x, quarter, 1) * tab_ref[base + 2])


def _proj_kernel(x_ref, mod_ref, g_ref, win_ref, qn_ref, kvn_ref, wuq_ref, wuk_ref, wuv_ref, rope_ref,
                 mq_ref, mk_ref, mv_ref, dq_ref, dk_ref, dv_ref, wq_ref, wk_ref, wv_ref,
                 nq_ref, nk_ref, nv_ref):
    mod = mod_ref[0]
    h = _adaln(x_ref[0], g_ref[...], mod[0:1], mod[1:2]).astype(BF16)

    def seg(start, width):
        return _dot(h, win_ref[:, start:start + width])

    def blocks(ref, val, n_blocks, rope_base=None, quarter=None):
        for i in range(n_blocks):
            blk = val[:, LANES * i:LANES * (i + 1)]
            if rope_base is not None:
                blk = _rope(blk, rope_ref, rope_base, quarter)
            ref[0, :, LANES * i:LANES * (i + 1)] = blk.astype(BF16)

    cq = _rms(seg(P_CQ, MLA_Q_RANK), qn_ref[...]).astype(BF16)
    blocks(mq_ref, _dot(cq, wuq_ref[...]), MLA_HEADS, 0, MLA_ROPE // 4)
    ckv = _rms(seg(P_CKV, MLA_KV_RANK), kvn_ref[...]).astype(BF16)
    kr = _rope(seg(P_KR, LANES), rope_ref, 0, MLA_ROPE // 4)
    kn = _dot(ckv, wuk_ref[...])
    for i in range(MLA_HEADS):
        mk_ref[0, :, LANES * i:LANES * (i + 1)] = (kn[:, LANES * i:LANES * (i + 1)] + kr).astype(BF16)
    mv_ref[0] = _dot(ckv, wuv_ref[...]).astype(BF16)
    blocks(dq_ref, seg(P_DQ, 256), 2, 3, DIFF_QK // 4)
    blocks(dk_ref, seg(P_DK, 256), 2, 3, DIFF_QK // 4)
    blocks(dv_ref, seg(P_DV, 256), 2)
    blocks(wq_ref, seg(P_WQ, 256), 2, 6, WIN_DIM // 4)
    blocks(wk_ref, seg(P_WK, 128), 1, 6, WIN_DIM // 4)
    blocks(wv_ref, seg(P_WV, 128), 1)
    blocks(nq_ref, seg(P_NQ, 256), 2)
    blocks(nk_ref, seg(P_NK, 256), 2)
    blocks(nv_ref, seg(P_NV, 256), 2)


_PROJ_OUT_WIDTHS = (512, 512, 256, 256, 256, 256, 256, 128, 128, 256, 256, 256)


def _project(xs, mods, g, w_in_p, q_norm, kv_norm, w_uq_p, w_uk_p, w_uv, rope_tab):
    tm = TM_PROJ
    n_lat_tiles = SEQ // tm
    n_tiles = T_ALL // tm
    tok = lambda width: pl.BlockSpec((1, tm, width), lambda j, b: (b, j, 0))
    return pl.pallas_call(
        _proj_kernel,
        grid=(n_tiles, BATCH),
        in_specs=[
            tok(D_MODEL),
            pl.BlockSpec((1, N_ADA, D_MODEL), _mod_index(n_lat_tiles)),
            _const_spec((1, D_MODEL)),
            _const_spec((D_MODEL, P_WIDTH)),
            _const_spec((1, MLA_Q_RANK)),
            _const_spec((1, MLA_KV_RANK)),
            _const_spec((MLA_Q_RANK, MLA_HEADS * LANES)),
            _const_spec((MLA_KV_RANK, MLA_HEADS * LANES)),
            _const_spec((MLA_KV_RANK, MLA_HEADS * MLA_V)),
            pl.BlockSpec((9, tm, LANES), lambda j, b: (0, j, 0)),
        ],
        out_specs=[tok(w) for w in _PROJ_OUT_WIDTHS],
        out_shape=[jax.ShapeDtypeStruct((BATCH, T_ALL, w), BF16) for w in _PROJ_OUT_WIDTHS],
        compiler_params=_params(2),
        name="project",
    )(xs, mods, g, w_in_p, q_norm, kv_norm, w_uq_p, w_uk_p, w_uv, rope_tab)


def _mla_kernel(q_ref, k_ref, v_ref, o_ref, *, n_lat_tiles, ctx_queries):
    lane = _lane_iota(TQ_DENSE)

    def attend(k0, nk):
        for blk in range(MLA_HEADS // 2):
            vb = v_ref[0, k0:k0 + nk, LANES * blk:LANES * (blk + 1)]
            pair = None
            for u in range(2):
                hd = 2 * blk + u
                s = _dot_nt(q_ref[0, :, LANES * hd:LANES * (hd + 1)],
                            k_ref[0, k0:k0 + nk, LANES * hd:LANES * (hd + 1)]) * MLA_SCALE
                (p,) = _softmax_parts([s])
                o = _dot(p.astype(BF16), vb)
                pair = o if u == 0 else jnp.where(lane < HALF, pair, o)
            o_ref[0, :, LANES * blk:LANES * (blk + 1)] = pair.astype(BF16)

    if ctx_queries:
        j = pl.program_id(1)
        pl.when(j < n_lat_tiles)(lambda: attend(0, T_ALL))
        pl.when(j >= n_lat_tiles)(lambda: attend(SEQ, CTX_LEN))
    else:
        attend(0, T_ALL)


def _diff_kernel(q_ref, k_ref, v_ref, lam_ref, sub_ref, o_ref, *, n_lat_tiles, ctx_queries, lam_init):
    lane = _lane_iota(TQ_DENSE)
    lp = lam_ref[...]
    lam = (jnp.exp(jnp.sum(lp[0:1] * lp[1:2], axis=-1, keepdims=True))
           - jnp.exp(jnp.sum(lp[2:3] * lp[3:4], axis=-1, keepdims=True)) + lam_init)

    def attend(k0, nk):
        for blk in range(DIFF_HEADS // 2):
            qb = q_ref[0, :, LANES * blk:LANES * (blk + 1)]
            kb = k_ref[0, k0:k0 + nk, LANES * blk:LANES * (blk + 1)]
            vb = v_ref[0, k0:k0 + nk, LANES * blk:LANES * (blk + 1)]
            pair = None
            for u in range(2):
                base = HALF * u
                q1 = jnp.where((lane >= base) & (lane < base + DIFF_QK), qb, jnp.zeros_like(qb))
                q2 = jnp.where((lane >= base + DIFF_QK) & (lane < base + 2 * DIFF_QK), qb, jnp.zeros_like(qb))
                (p1,) = _softmax_parts([_dot_nt(q1, kb) * DIFF_SCALE])
                (p2,) = _softmax_parts([_dot_nt(q2, kb) * DIFF_SCALE])
                o = _dot((p1 - lam * p2).astype(BF16), vb)
                pair = o if u == 0 else jnp.where(lane < HALF, pair, o)
            sq = pair * pair
            lo = jnp.sum(jnp.where(lane < HALF, sq, 0.0), axis=-1, keepdims=True)
            hi = jnp.sum(jnp.where(lane < HALF, 0.0, sq), axis=-1, keepdims=True)
            ms = jnp.where(lane < HALF, lo, hi) * (1.0 / DIFF_V)
            y = pair * lax.rsqrt(ms + EPS) * sub_ref[...] * (1.0 - lam_init)
            o_ref[0, :, LANES * blk:LANES * (blk + 1)] = y.astype(BF16)

    if ctx_queries:
        j = pl.program_id(1)
        pl.when(j < n_lat_tiles)(lambda: attend(0, T_ALL))
        pl.when(j >= n_lat_tiles)(lambda: attend(SEQ, CTX_LEN))
    else:
        attend(0, T_ALL)


def _dense_attention(body, q, k, v, extra, extra_specs, ctx_queries, name):
    tq = TQ_DENSE
    n_lat_tiles = SEQ // tq
    n_tiles = (T_ALL if ctx_queries else SEQ) // tq
    kw, vw, qw = k.shape[-1], v.shape[-1], q.shape[-1]
    return pl.pallas_call(
        functools.partial(body, n_lat_tiles=n_lat_tiles, ctx_queries=ctx_queries),
        grid=(BATCH, n_tiles),
        in_specs=[
            pl.BlockSpec((1, tq, qw), lambda b, j: (b, j, 0)),
            pl.BlockSpec((1, T_ALL, kw), lambda b, j: (b, 0, 0)),
            pl.BlockSpec((1, T_ALL, vw), lambda b, j: (b, 0, 0)),
        ] + extra_specs,
        out_specs=pl.BlockSpec((1, tq, BRANCH_W), lambda b, j: (b, j, 0)),
        out_shape=jax.ShapeDtypeStruct((BATCH, n_tiles * tq, BRANCH_W), BF16),
        compiler_params=_params(2),
        name=name,
    )(q, k, v, *extra)


def _win_kernel(sink_ref, q_ref, k_ref, v_ref, o_ref, *, n_lat_tiles, ctx_queries):
    j = pl.program_id(1)
    lane = _lane_iota(TQ_WIN)
    n_loc = 3 * WINDOW

    def heads(loc):
        kc = k_ref[0, SEQ:T_ALL, :]
        vc = v_ref[0, SEQ:T_ALL, :]
        if loc is not None:
            start, allowed = loc
            kl = k_ref[0, pl.ds(start, n_loc), :]
            vl = v_ref[0, pl.ds(start, n_loc), :]
        for blk in range(2):
            qb = q_ref[0, :, LANES * blk:LANES * (blk + 1)]
            pair = None
            for u in range(2):
                qm = jnp.where((lane >= HALF * u) & (lane < HALF * (u + 1)), qb, jnp.zeros_like(qb))
                sink = sink_ref[blk + 2 * u]
                s_ctx = _dot_nt(qm, kc) * WIN_SCALE
                if loc is not None:
                    s_loc = jnp.where(allowed, _dot_nt(qm, kl) * WIN_SCALE, NEG_INF)
                    p_loc, p_ctx = _softmax_parts([s_loc, s_ctx], sink)
                    o = _dot(p_loc.astype(BF16), vl) + _dot(p_ctx.astype(BF16), vc)
                else:
                    (p_ctx,) = _softmax_parts([s_ctx], sink)
                    o = _dot(p_ctx.astype(BF16), vc)
                pair = o if u == 0 else jnp.where(lane < HALF, pair, o)
            o_ref[0, :, LANES * blk:LANES * (blk + 1)] = pair.astype(BF16)

    def latent():
        start = pl.multiple_of(jnp.clip((j - 1) * WINDOW, 0, SEQ - n_loc), WINDOW)
        qpos = j * WINDOW + lax.broadcasted_iota(jnp.int32, (TQ_WIN, n_loc), 0)
        kpos = start + lax.broadcasted_iota(jnp.int32, (TQ_WIN, n_loc), 1)
        heads((start, jnp.abs(qpos - kpos) <= WINDOW))

    if ctx_queries:
        pl.when(j < n_lat_tiles)(latent)
        pl.when(j >= n_lat_tiles)(lambda: heads(None))
    else:
        latent()


def _window_attention(q, k, v, sink, ctx_queries):
    tq = TQ_WIN
    n_lat_tiles = SEQ // tq
    n_tiles = (T_ALL if ctx_queries else SEQ) // tq
    return pl.pallas_call(
        functools.partial(_win_kernel, n_lat_tiles=n_lat_tiles, ctx_queries=ctx_queries),
        grid=(BATCH, n_tiles),
        in_specs=[
            pl.BlockSpec(memory_space=pltpu.SMEM),
            pl.BlockSpec((1, tq, BRANCH_W), lambda b, j: (b, j, 0)),
            pl.BlockSpec((1, T_ALL, LANES), lambda b, j: (b, 0, 0)),
            pl.BlockSpec((1, T_ALL, LANES), lambda b, j: (b, 0, 0)),
        ],
        out_specs=pl.BlockSpec((1, tq, BRANCH_W), lambda b, j: (b, j, 0)),
        out_shape=jax.ShapeDtypeStruct((BATCH, n_tiles * tq, BRANCH_W), BF16),
        compiler_params=_params(2),
        name="window_attention",
    )(sink, q, k, v)


def _na_kernel(q_ref, k_ref, v_ref, bias_ref, o_ref, *, n_lat_tiles, ctx_queries):
    j = pl.program_id(1)
    lane = _lane_iota(TQ_NA)
    n_loc = NA_ROWS * GRID_W

    def heads(loc):
        for blk in range(NA_HEADS // 2):
            qb = q_ref[0, :, LANES * blk:LANES * (blk + 1)]
            kc = k_ref[0, SEQ:T_ALL, LANES * blk:LANES * (blk + 1)]
            vc = v_ref[0, SEQ:T_ALL, LANES * blk:LANES * (blk + 1)]
            if loc is not None:
                start, variant = loc
                kl = k_ref[0, pl.ds(start, n_loc), LANES * blk:LANES * (blk + 1)]
                vl = v_ref[0, pl.ds(start, n_loc), LANES * blk:LANES * (blk + 1)]
            pair = None
            for u in range(2):
                qm = jnp.where((lane >= HALF * u) & (lane < HALF * (u + 1)), qb, jnp.zeros_like(qb))
                s_ctx = _dot_nt(qm, kc) * NA_SCALE
                if loc is not None:
                    s_loc = _dot_nt(qm, kl) * NA_SCALE + bias_ref[variant, 2 * blk + u]
                    p_loc, p_ctx = _softmax_parts([s_loc, s_ctx])
                    o = _dot(p_loc.astype(BF16), vl) + _dot(p_ctx.astype(BF16), vc)
                else:
                    (p_ctx,) = _softmax_parts([s_ctx])
                    o = _dot(p_ctx.astype(BF16), vc)
                pair = o if u == 0 else jnp.where(lane < HALF, pair, o)
            o_ref[0, :, LANES * blk:LANES * (blk + 1)] = pair.astype(BF16)

    def latent():
        first = jnp.clip(j - NA_ROWS // 2, 0, GRID_ROWS - NA_ROWS)
        heads((pl.multiple_of(first * GRID_W, GRID_W), first - j + NA_ROWS - 1))

    if ctx_queries:
        pl.when(j < n_lat_tiles)(latent)
        pl.when(j >= n_lat_tiles)(lambda: heads(None))
    else:
        latent()


def _na_bias_table(rpb):
    variant = np.arange(NA_ROWS)[:, None]
    d_row = variant + np.arange(NA_ROWS)[None, :]
    qc = np.arange(GRID_W)[:, None]
    kc = np.arange(GRID_W)[None, :]
    d_col = np.clip(kc - qc, -(NA_COLS - 1), NA_COLS - 1) + (NA_COLS - 1)
    c_start = np.clip(qc - NA_COLS // 2, 0, GRID_W - NA_COLS)
    in_win = (kc >= c_start) & (kc < c_start + NA_COLS)
    tab = rpb.astype(F32)[:, d_row[:, None, :, None], d_col[None, :, None, :]]
    tab = jnp.where(in_win[None, None, :, None, :], tab, NEG_INF)
    return jnp.transpose(tab, (1, 0, 2, 3, 4)).reshape(NA_ROWS, NA_HEADS, GRID_W, NA_ROWS * GRID_W)


def _na_attention(q, k, v, bias, ctx_queries):
    tq = TQ_NA
    n_lat_tiles = SEQ // tq
    n_tiles = (T_ALL if ctx_queries else SEQ) // tq
    return pl.pallas_call(
        functools.partial(_na_kernel, n_lat_tiles=n_lat_tiles, ctx_queries=ctx_queries),
        grid=(BATCH, n_tiles),
        in_specs=[
            pl.BlockSpec((1, tq, BRANCH_W), lambda b, j: (b, j, 0)),
            pl.BlockSpec((1, T_ALL, BRANCH_W), lambda b, j: (b, 0, 0)),
            pl.BlockSpec((1, T_ALL, BRANCH_W), lambda b, j: (b, 0, 0)),
            _const_spec(bias.shape),
        ],
        out_specs=pl.BlockSpec((1, tq, BRANCH_W), lambda b, j: (b, j, 0)),
        out_shape=jax.ShapeDtypeStruct((BATCH, n_tiles * tq, BRANCH_W), BF16),
        compiler_params=_params(2),
        name="neighbourhood_attention",
    )(q, k, v, bias)


def _merge_kernel(x_ref, mod_ref, g_ref, oa_ref, ob_ref, oc_ref, od_ref, wg_ref, bg_ref, wb_ref, wo_ref, out_ref):
    x = x_ref[0]
    mod = mod_ref[0]
    h = _adaln(x, g_ref[...], mod[0:1], mod[1:2]).astype(BF16)
    y = None
    for i, o_ref in enumerate((oa_ref, ob_ref, oc_ref, od_ref)):
        cols = slice(D_MODEL * i, D_MODEL * (i + 1))
        gate = jax.nn.sigmoid(_dot(h, wg_ref[:, cols]) + bg_ref[:, cols])
        term = gate * _dot(o_ref[0], wb_ref[BRANCH_W * i:BRANCH_W * (i + 1), :])
        y = term if y is None else y + term
    out_ref[0] = x + mod[2:3] * _dot(y.astype(BF16), wo_ref[...])


def _mlp_kernel(x_ref, mod_ref, g_ref, w1_ref, w2_ref, fn_ref, out_ref, *, final):
    x = x_ref[0]
    mod = mod_ref[0]
    h = _adaln(x, g_ref[...], mod[3:4], mod[4:5]).astype(BF16)
    acc = None
    for i in range(FFN_HIDDEN // D_MODEL):
        cols = slice(D_MODEL * i, D_MODEL * (i + 1))
        u = jnp.square(jnp.maximum(_dot(h, w1_ref[:, cols]), 0.0)).astype(BF16)
        part = _dot(u, w2_ref[cols, :])
        acc = part if acc is None else acc + part
    out = x + mod[5:6] * acc
    out_ref[0] = _rms(out, fn_ref[...]) if final else out


def _token_grid(with_ctx):
    tm = TM_MLP
    n_lat_tiles = SEQ // tm
    n_tiles = (T_ALL if with_ctx else SEQ) // tm
    tok = lambda width: pl.BlockSpec((1, tm, width), lambda j, b: (b, j, 0))
    mod = pl.BlockSpec((1, N_ADA, D_MODEL), _mod_index(n_lat_tiles))
    return tm, n_tiles, tok, mod


def _merge(xs, mods, g, outs, w_gate, b_gate, w_branch, w_out, with_ctx):
    tm, n_tiles, tok, mod = _token_grid(with_ctx)
    rows = T_ALL if with_ctx else SEQ
    return pl.pallas_call(
        _merge_kernel,
        grid=(n_tiles, BATCH),
        in_specs=[tok(D_MODEL), mod, _const_spec((1, D_MODEL))] + [tok(BRANCH_W)] * N_BRANCH + [
            _const_spec((D_MODEL, N_BRANCH * D_MODEL)),
            _const_spec((1, N_BRANCH * D_MODEL)),
            _const_spec((N_BRANCH * BRANCH_W, D_MODEL)),
            _const_spec((D_MODEL, D_MODEL)),
        ],
        out_specs=tok(D_MODEL),
        out_shape=jax.ShapeDtypeStruct((BATCH, rows, D_MODEL), F32),
        compiler_params=_params(2),
        name="merge",
    )(xs, mods, g, *outs, w_gate, b_gate, w_branch, w_out)


def _mlp(xs, mods, g, w1, w2, final_norm, with_ctx, final):
    tm, n_tiles, tok, mod = _token_grid(with_ctx)
    rows = T_ALL if with_ctx else SEQ
    return pl.pallas_call(
        functools.partial(_mlp_kernel, final=final),
        grid=(n_tiles, BATCH),
        in_specs=[tok(D_MODEL), mod, _const_spec((1, D_MODEL)),
                  _const_spec((D_MODEL, FFN_HIDDEN)), _const_spec((FFN_HIDDEN, D_MODEL)),
                  _const_spec((1, D_MODEL))],
        out_specs=tok(D_MODEL),
        out_shape=jax.ShapeDtypeStruct((BATCH, rows, D_MODEL), F32),
        compiler_params=_params(2),
        name="mlp_final" if final else "mlp",
    )(xs, mods, g, w1, w2, final_norm)


def _rope_tables():
    t = jnp.arange(SEQ, dtype=jnp.int32)
    row = (t // GRID_W).astype(F32)[:, None]
    col = (t % GRID_W).astype(F32)[:, None]

    def table(dim, lane_start, lane_stop):
        quarter = dim // 4
        inv_freq = ROPE_BASE ** (-jnp.arange(quarter, dtype=F32) / quarter)
        ang = jnp.concatenate([row * inv_freq, row * inv_freq, col * inv_freq, col * inv_freq], axis=-1)
        reps = (lane_stop - lane_start) // dim
        cos = jnp.tile(jnp.cos(ang), (1, reps))
        sin = jnp.tile(jnp.sin(ang), (1, reps))
        lower = (np.arange(lane_stop - lane_start) % (2 * quarter)) < quarter
        place = lambda base, a: base.at[:SEQ, lane_start:lane_stop].set(a)
        zeros = jnp.zeros((T_ALL, LANES), F32)
        return [place(jnp.ones((T_ALL, LANES), F32), cos),
                place(zeros, jnp.where(lower, -sin, 0.0)),
                place(zeros, jnp.where(lower, 0.0, sin))]

    tabs = table(MLA_ROPE, MLA_NOPE, MLA_NOPE + MLA_ROPE) + table(DIFF_QK, 0, LANES) + table(WIN_DIM, 0, LANES)
    return jnp.stack(tabs)


def _pad_heads(w, heads, width, keep):
    w = w.reshape(w.shape[0], heads, width)[:, :, keep[0]:keep[1]]
    return jnp.pad(w, ((0, 0), (0, 0), (0, LANES - (keep[1] - keep[0])))).reshape(w.shape[0], heads * LANES)


_WIN_HEAD_ORDER = (0, 2, 1, 3)


def _prep_w_in(w):
    seg = lambda a, b: w[:, a:b]
    kr = jnp.pad(seg(384, 416), ((0, 0), (MLA_NOPE, LANES - MLA_NOPE - MLA_ROPE)))
    wq = seg(1184, 1440).reshape(D_MODEL, WIN_HEADS, WIN_DIM)[:, np.array(_WIN_HEAD_ORDER)].reshape(D_MODEL, 256)
    return jnp.concatenate([seg(0, 384), kr, seg(416, 1184), wq, seg(1440, 2464)], axis=1).astype(BF16)


def _prep_w_branch(w):
    win = w[512:768].reshape(WIN_HEADS, WIN_DIM, D_MODEL)[np.array(_WIN_HEAD_ORDER)].reshape(256, D_MODEL)
    return jnp.concatenate([w[:512], win, w[768:]], axis=0).astype(BF16)


def _lambda_init(layer):
    return 0.8 - 0.6 * math.exp(-0.3 * layer)


def kernel(x, c, ctx, c_ctx, w_ada, b_ada, norm_mix, norm_ffn, w_in, mla_q_norm, mla_kv_norm, w_uq, w_ukv,
           diff_lambda, diff_subln, win_sink, na_rpb, w_gate, b_gate, w_branch, w_out, w_ff1, w_ff2, final_norm):
    cvec = jnp.concatenate([c, c_ctx[None, :], jnp.zeros((N_MOD_ROWS - BATCH - 1, D_MODEL), F32)], axis=0)
    mods = _modulation(cvec, w_ada, b_ada)
    rope_tab = _rope_tables()
    xs = jnp.concatenate([x, ctx], axis=1)
    row = lambda v: v.reshape(1, -1)
    for l in range(DEPTH):
        need_ctx = l < DEPTH - 1
        ukv = w_ukv[l].reshape(MLA_KV_RANK, MLA_HEADS, MLA_NOPE + MLA_V)
        proj = _project(
            xs, mods[l], row(norm_mix[l]), _prep_w_in(w_in[l]), row(mla_q_norm[l]), row(mla_kv_norm[l]),
            _pad_heads(w_uq[l], MLA_HEADS, MLA_NOPE + MLA_ROPE, (0, MLA_NOPE + MLA_ROPE)).astype(BF16),
            _pad_heads(w_ukv[l], MLA_HEADS, MLA_NOPE + MLA_V, (0, MLA_NOPE)).astype(BF16),
            ukv[:, :, MLA_NOPE:].reshape(MLA_KV_RANK, MLA_HEADS * MLA_V).astype(BF16),
            rope_tab)
        mq, mk, mv, dq, dk, dv, wq, wk, wv, nq, nk, nv = proj
        oa = _dense_attention(_mla_kernel, mq, mk, mv, (), [], need_ctx, "mla_attention")
        ob = _dense_attention(
            functools.partial(_diff_kernel, lam_init=_lambda_init(l)), dq, dk, dv,
            (diff_lambda[l], jnp.tile(diff_subln[l], 2).reshape(1, LANES)),
            [_const_spec((4, DIFF_QK)), _const_spec((1, LANES))], need_ctx, "diff_attention")
        oc = _window_attention(wq, wk, wv, win_sink[l], need_ctx)
        od = _na_attention(nq, nk, nv, _na_bias_table(na_rpb[l]), need_ctx)
        xs = _merge(xs, mods[l], row(norm_mix[l]), (oa, ob, oc, od), w_gate[l].astype(BF16), row(b_gate[l]),
                    _prep_w_branch(w_branch[l]), w_out[l].astype(BF16), need_ctx)
        xs = _mlp(xs, mods[l], row(norm_ffn[l]), w_ff1[l].astype(BF16), w_ff2[l].astype(BF16),
                  row(final_norm), need_ctx, final=not need_ctx)
    return xs
```

```python
import functools
import math

import numpy as np
import jax
import jax.numpy as jnp
from jax import lax
from jax.experimental import pallas as pl
from jax.experimental.pallas import tpu as pltpu

D_MODEL = 1024
BATCH = 8
SEQ = 2048
DEPTH = 2
CTX_LEN = 256
GRID_W = 64
GRID_ROWS = SEQ // GRID_W
ROPE_BASE = 10000.0
NEG_INF = -1e30
EPS = 1e-6
N_ADA = 6
LOG2E = math.log2(math.e)

MLA_HEADS = 4
MLA_NOPE = 64
MLA_ROPE = 32
MLA_V = 64
MLA_Q_RANK = 256
MLA_KV_RANK = 128
MLA_SCALE = (MLA_NOPE + MLA_ROPE) ** -0.5
DIFF_HEADS = 4
DIFF_QK = 32
DIFF_V = 64
DIFF_SCALE = DIFF_QK ** -0.5
WIN_HEADS = 4
WIN_KV_HEADS = 2
WIN_DIM = 64
WINDOW = 128
WIN_SCALE = WIN_DIM ** -0.5
NA_HEADS = 4
NA_DIM = 64
NA_ROWS = 8
NA_COLS = 16
NA_SCALE = NA_DIM ** -0.5
FFN_HIDDEN = 4 * D_MODEL
N_BRANCH = 4
BRANCH_W = 256

LANES = 128
HALF = LANES // 2
VMEM_LIMIT = 56 * 1024 * 1024

BF16 = jnp.bfloat16
F32 = jnp.float32

P_CQ, P_CKV, P_KR = 0, 256, 384
P_DQ, P_DK, P_DV = 512, 768, 1024
P_WQ, P_WK, P_WV = 1280, 1536, 1664
P_NQ, P_NK, P_NV = 1792, 2048, 2304
P_WIDTH = 2560

TM_PROJ = 256
TQ = 256
TM_LAT = 512
N_MOD_ROWS = 16
CTX_MOD_ROW = BATCH

NA_STRIP = TQ // GRID_W
NA_KEY_ROWS = 12
NA_N_LOC = NA_KEY_ROWS * GRID_W
WIN_N_LOC = TQ + 2 * WINDOW


def _const_spec(shape):
    nd = len(shape)
    return pl.BlockSpec(shape, lambda *_: (0,) * nd, pipeline_mode=pl.Buffered(1))


def _params(n_axes):
    return pltpu.CompilerParams(dimension_semantics=("arbitrary",) * n_axes, vmem_limit_bytes=VMEM_LIMIT)


def _rms(x, g):
    return x * lax.rsqrt(jnp.mean(x * x, axis=-1, keepdims=True) + EPS) * g


def _adaln(x, g, shift, scale):
    return _rms(x, g) * (1.0 + scale) + shift


def _dot(a, b):
    return jnp.dot(a, b, preferred_element_type=F32)


def _dot_nt(a, b):
    return lax.dot_general(a, b, (((1,), (1,)), ((), ())), preferred_element_type=F32)


def _lane_iota(rows):
    return lax.broadcasted_iota(jnp.int32, (rows, LANES), 1)


def _blk(i):
    return slice(LANES * i, LANES * (i + 1))


def _mod_kernel(c_ref, w_ref, b_ref, o_ref):
    c = c_ref[...]
    act = c * jax.nn.sigmoid(c)
    o_ref[0] = jnp.dot(act, w_ref[0], precision=lax.Precision.HIGHEST, preferred_element_type=F32) + b_ref[0]


def _modulation(cvec, w_ada, b_ada):
    tn = 1536
    n_ada = N_ADA * D_MODEL
    out = pl.pallas_call(
        _mod_kernel,
        grid=(DEPTH, n_ada // tn),
        in_specs=[
            pl.BlockSpec((N_MOD_ROWS, D_MODEL), lambda l, n: (0, 0)),
            pl.BlockSpec((1, D_MODEL, tn), lambda l, n: (l, 0, n)),
            pl.BlockSpec((1, 1, tn), lambda l, n: (l, 0, n)),
        ],
        out_specs=pl.BlockSpec((1, N_MOD_ROWS, tn), lambda l, n: (l, 0, n)),
        out_shape=jax.ShapeDtypeStruct((DEPTH, N_MOD_ROWS, n_ada), F32),
        compiler_params=_params(2),
        name="modulation",
    )(cvec, w_ada, b_ada.reshape(DEPTH, 1, n_ada))
    return out.reshape(DEPTH, N_MOD_ROWS, N_ADA, D_MODEL)


def _mod_spec(is_ctx):
    if is_ctx:
        return pl.BlockSpec((1, N_ADA, D_MODEL), lambda j, b: (CTX_MOD_ROW, 0, 0))
    return pl.BlockSpec((1, N_ADA, D_MODEL), lambda j, b: (b, 0, 0))


def _proj_kernel(x_ref, mod_ref, g_ref, win_ref, qn_ref, kvn_ref, wuq_ref, wuk_ref, wuv_ref, *rest, rotary):
    if rotary:
        rope_ref, rest = rest[0], rest[1:]
    mq_ref, mk_ref, mv_ref, dq_ref, dk_ref, dv_ref, wq_ref, wk_ref, wv_ref, nq_ref, nk_ref, nv_ref = rest
    mod = mod_ref[0]
    h = _adaln(x_ref[0], g_ref[...], mod[0:1], mod[1:2]).astype(BF16)
    lane = _lane_iota(h.shape[0])

    def seg(start, width):
        return _dot(h, win_ref[:, start:start + width])

    def rope(x, base, quarter):
        if not rotary:
            return x
        return (x * rope_ref[base]
                + pltpu.roll(x, LANES - quarter, 1) * rope_ref[base + 1]
                + pltpu.roll(x, quarter, 1) * rope_ref[base + 2])

    def put(ref, val, rope_base=None, quarter=None, scale=None):
        for i in range(val.shape[1] // LANES):
            blk = val[:, _blk(i)]
            if rope_base is not None:
                blk = rope(blk, rope_base, quarter)
            if scale is not None:
                blk = blk * scale
            ref[0, :, _blk(i)] = blk.astype(BF16)

    def put_values(ref, val):
        for hd in range(val.shape[1] // HALF):
            blk = val[:, _blk(hd // 2)]
            if hd % 2:
                blk = pltpu.roll(blk, HALF, 1)
            ref[0, :, _blk(hd)] = jnp.where(lane < HALF, blk, 1.0).astype(BF16)

    cq = _rms(seg(P_CQ, MLA_Q_RANK), qn_ref[...]).astype(BF16)
    put(mq_ref, _dot(cq, wuq_ref[...]), 0, MLA_ROPE // 4, MLA_SCALE * LOG2E)
    ckv = _rms(seg(P_CKV, MLA_KV_RANK), kvn_ref[...]).astype(BF16)
    kr = rope(seg(P_KR, LANES), 0, MLA_ROPE // 4)
    kn = _dot(ckv, wuk_ref[...])
    for i in range(MLA_HEADS):
        mk_ref[0, :, _blk(i)] = (kn[:, _blk(i)] + kr).astype(BF16)
    put_values(mv_ref, _dot(ckv, wuv_ref[...]))
    put(dq_ref, seg(P_DQ, 256), 3, DIFF_QK // 4, DIFF_SCALE * LOG2E)
    put(dk_ref, seg(P_DK, 256), 3, DIFF_QK // 4)
    put_values(dv_ref, seg(P_DV, 256))
    put(wq_ref, seg(P_WQ, 256), 6, WIN_DIM // 4, WIN_SCALE * LOG2E)
    put(wk_ref, seg(P_WK, 128), 6, WIN_DIM // 4)
    put_values(wv_ref, seg(P_WV, 128))
    put(nq_ref, seg(P_NQ, 256), scale=NA_SCALE * LOG2E)
    put(nk_ref, seg(P_NK, 256))
    put_values(nv_ref, seg(P_NV, 256))


_PROJ_OUT_WIDTHS = (512, 512, 512, 256, 256, 512, 256, 128, 256, 256, 256, 512)


def _project(xs, mods, g, w_in_p, q_norm, kv_norm, w_uq_p, w_uk_p, w_uv, rope_tab, is_ctx):
    tm = TM_PROJ
    n_tok = xs.shape[1]
    tok = lambda width: pl.BlockSpec((1, tm, width), lambda j, b: (b, j, 0))
    in_specs = [
        tok(D_MODEL),
        _mod_spec(is_ctx),
        _const_spec((1, D_MODEL)),
        _const_spec((D_MODEL, P_WIDTH)),
        _const_spec((1, MLA_Q_RANK)),
        _const_spec((1, MLA_KV_RANK)),
        _const_spec((MLA_Q_RANK, MLA_HEADS * LANES)),
        _const_spec((MLA_KV_RANK, MLA_HEADS * LANES)),
        _const_spec((MLA_KV_RANK, MLA_HEADS * MLA_V)),
    ]
    args = [xs, mods, g, w_in_p, q_norm, kv_norm, w_uq_p, w_uk_p, w_uv]
    if not is_ctx:
        in_specs.append(pl.BlockSpec((9, tm, LANES), lambda j, b: (0, j, 0)))
        args.append(rope_tab)
    return pl.pallas_call(
        functools.partial(_proj_kernel, rotary=not is_ctx),
        grid=(n_tok // tm, BATCH),
        in_specs=in_specs,
        out_specs=[tok(w) for w in _PROJ_OUT_WIDTHS],
        out_shape=[jax.ShapeDtypeStruct((BATCH, n_tok, w), BF16) for w in _PROJ_OUT_WIDTHS],
        compiler_params=_params(2),
        name="project_ctx" if is_ctx else "project",
    )(*args)


def _attend(q, ks, vs, lane, post=None, sink=None):
    ss = []
    for i, k in enumerate(ks):
        s = _dot_nt(q, k)
        if post is not None and post[i] is not None:
            s = post[i](s)
        ss.append(s)
    m = ss[0].max(axis=-1, keepdims=True)
    for s in ss[1:]:
        m = jnp.maximum(m, s.max(axis=-1, keepdims=True))
    if sink is not None:
        m = jnp.maximum(m, sink)
    acc = None
    for s, v in zip(ss, vs):
        part = _dot(jnp.exp2(s - m).astype(BF16), v)
        acc = part if acc is None else acc + part
    if sink is not None:
        acc = acc + jnp.where(lane >= HALF, jnp.exp2(sink - m), 0.0)
    return acc


def _normalise(acc):
    return acc * (1.0 / pltpu.roll(acc, HALF, 1))


def _pair(lane, even, odd):
    return jnp.where(lane < HALF, even, pltpu.roll(odd, HALF, 1))


def _half_mask(lane, q, start, width):
    return jnp.where((lane >= start) & (lane < start + width), q, jnp.zeros_like(q))


def _mla_heads(q_ref, kv_parts, o_ref, lane):
    for blk in range(MLA_HEADS // 2):
        outs = []
        for hd in (2 * blk, 2 * blk + 1):
            acc = _attend(q_ref[0, :, _blk(hd)], [k[0, :, _blk(hd)] for k, _ in kv_parts],
                          [v[0, :, _blk(hd)] for _, v in kv_parts], lane)
            outs.append(_normalise(acc))
        o_ref[0, :, _blk(blk)] = _pair(lane, *outs).astype(BF16)


def _diff_lambda(lam_ref, lam_init):
    lp = lam_ref[...]
    return (jnp.exp(jnp.sum(lp[0:1] * lp[1:2], axis=-1, keepdims=True))
            - jnp.exp(jnp.sum(lp[2:3] * lp[3:4], axis=-1, keepdims=True)) + lam_init)


def _diff_heads(q_ref, kv_parts, lam_ref, sub_ref, o_ref, lane, lam_init):
    lam = _diff_lambda(lam_ref, lam_init)
    for blk in range(DIFF_HEADS // 2):
        qb = q_ref[0, :, _blk(blk)]
        ks = [k[0, :, _blk(blk)] for k, _ in kv_parts]
        outs = []
        for u in range(2):
            vs = [v[0, :, _blk(2 * blk + u)] for _, v in kv_parts]
            a1 = _attend(_half_mask(lane, qb, HALF * u, DIFF_QK), ks, vs, lane)
            a2 = _attend(_half_mask(lane, qb, HALF * u + DIFF_QK, DIFF_QK), ks, vs, lane)
            outs.append(_normalise(a1) - lam * _normalise(a2))
        pair = _pair(lane, *outs)
        sq = pair * pair
        lo = jnp.sum(jnp.where(lane < HALF, sq, 0.0), axis=-1, keepdims=True)
        hi = jnp.sum(jnp.where(lane < HALF, 0.0, sq), axis=-1, keepdims=True)
        ms = jnp.where(lane < HALF, lo, hi) * (1.0 / DIFF_V)
        o_ref[0, :, _blk(blk)] = (pair * lax.rsqrt(ms + EPS) * sub_ref[...] * (1.0 - lam_init)).astype(BF16)


def _win_heads(q_ref, k_parts, v_parts, posts, sink_ref, o_ref, lane):
    for blk in range(2):
        qb = q_ref[0, :, _blk(blk)]
        outs = []
        for u in range(WIN_KV_HEADS):
            acc = _attend(_half_mask(lane, qb, HALF * u, HALF), k_parts, [v[:, _blk(u)] for v in v_parts], lane,
                          post=posts, sink=sink_ref[blk + 2 * u] * LOG2E)
            outs.append(_normalise(acc))
        o_ref[0, :, _blk(blk)] = _pair(lane, *outs).astype(BF16)


def _na_heads(q_ref, kv_parts, posts_of_head, o_ref, lane):
    for blk in range(NA_HEADS // 2):
        qb = q_ref[0, :, _blk(blk)]
        outs = []
        for u in range(2):
            hd = 2 * blk + u
            acc = _attend(_half_mask(lane, qb, HALF * u, HALF), [k(_blk(blk)) for k, _ in kv_parts],
                          [v(_blk(hd)) for _, v in kv_parts], lane, post=posts_of_head(hd))
            outs.append(_normalise(acc))
        o_ref[0, :, _blk(blk)] = _pair(lane, *outs).astype(BF16)


def _mla_kernel(q_ref, kl_ref, vl_ref, kc_ref, vc_ref, o_ref):
    _mla_heads(q_ref, [(kl_ref, vl_ref), (kc_ref, vc_ref)], o_ref, _lane_iota(TQ))


def _diff_kernel(q_ref, kl_ref, vl_ref, kc_ref, vc_ref, lam_ref, sub_ref, o_ref, *, lam_init):
    _diff_heads(q_ref, [(kl_ref, vl_ref), (kc_ref, vc_ref)], lam_ref, sub_ref, o_ref, _lane_iota(TQ), lam_init)


def _win_kernel(sink_ref, q_ref, kl_ref, vl_ref, kc_ref, vc_ref, o_ref):
    j = pl.program_id(1)
    start = pl.multiple_of(jnp.clip(j * TQ - WINDOW, 0, SEQ - WIN_N_LOC), WINDOW)
    qpos = j * TQ + lax.broadcasted_iota(jnp.int32, (TQ, WIN_N_LOC), 0)
    kpos = start + lax.broadcasted_iota(jnp.int32, (TQ, WIN_N_LOC), 1)
    allowed = jnp.abs(qpos - kpos) <= WINDOW
    band = lambda s: jnp.where(allowed, s, NEG_INF)
    _win_heads(q_ref, [kl_ref[0, pl.ds(start, WIN_N_LOC), :], kc_ref[0]],
               [vl_ref[0, pl.ds(start, WIN_N_LOC), :], vc_ref[0]], [band, None], sink_ref, o_ref, _lane_iota(TQ))


def _na_kernel(q_ref, kl_ref, vl_ref, kc_ref, vc_ref, bias_ref, o_ref):
    j = pl.program_id(1)
    n_strips = GRID_ROWS // NA_STRIP
    first = jnp.clip(j * NA_STRIP - NA_ROWS // 2, 0, GRID_ROWS - NA_KEY_ROWS)
    start = pl.multiple_of(first * GRID_W, GRID_W)
    variant = (j >= 1).astype(jnp.int32) + (j >= 2).astype(jnp.int32) + (j >= n_strips - 1).astype(jnp.int32)
    parts = [(lambda ls: kl_ref[0, pl.ds(start, NA_N_LOC), ls], lambda ls: vl_ref[0, pl.ds(start, NA_N_LOC), ls]),
             (lambda ls: kc_ref[0, :, ls], lambda ls: vc_ref[0, :, ls])]
    posts = lambda hd: [lambda s: s + bias_ref[variant, hd], None]
    _na_heads(q_ref, parts, posts, o_ref, _lane_iota(TQ))


def _latent_attention(body, q, kl, vl, kc, vc, name, pre=(), pre_specs=(), post=(), post_specs=()):
    qw, kw, vw = q.shape[-1], kl.shape[-1], vl.shape[-1]
    return pl.pallas_call(
        body,
        grid=(BATCH, SEQ // TQ),
        in_specs=list(pre_specs) + [
            pl.BlockSpec((1, TQ, qw), lambda b, j: (b, j, 0)),
            pl.BlockSpec((1, SEQ, kw), lambda b, j: (b, 0, 0)),
            pl.BlockSpec((1, SEQ, vw), lambda b, j: (b, 0, 0)),
            pl.BlockSpec((1, CTX_LEN, kw), lambda b, j: (b, 0, 0)),
            pl.BlockSpec((1, CTX_LEN, vw), lambda b, j: (b, 0, 0)),
        ] + list(post_specs),
        out_specs=pl.BlockSpec((1, TQ, BRANCH_W), lambda b, j: (b, j, 0)),
        out_shape=jax.ShapeDtypeStruct((BATCH, SEQ, BRANCH_W), BF16),
        compiler_params=_params(2),
        name=name,
    )(*pre, q, kl, vl, kc, vc, *post)


def _ctx_kernel(sink_ref, mq, mk, mv, dq, dk, dv, wq, wk, wv, nq, nk, nv, lam_ref, sub_ref,
                oa_ref, ob_ref, oc_ref, od_ref, *, lam_init):
    lane = _lane_iota(CTX_LEN)
    _mla_heads(mq, [(mk, mv)], oa_ref, lane)
    _diff_heads(dq, [(dk, dv)], lam_ref, sub_ref, ob_ref, lane, lam_init)
    _win_heads(wq, [wk[0]], [wv[0]], None, sink_ref, oc_ref, lane)
    _na_heads(nq, [(lambda ls: nk[0, :, ls], lambda ls: nv[0, :, ls])], lambda hd: None, od_ref, lane)


def _ctx_attention(proj_ctx, sink, lam, sub, lam_init):
    tok = lambda width: pl.BlockSpec((1, CTX_LEN, width), lambda b: (b, 0, 0))
    return pl.pallas_call(
        functools.partial(_ctx_kernel, lam_init=lam_init),
        grid=(BATCH,),
        in_specs=[pl.BlockSpec(memory_space=pltpu.SMEM)] + [tok(w) for w in _PROJ_OUT_WIDTHS]
        + [_const_spec((4, DIFF_QK)), _const_spec((1, LANES))],
        out_specs=[tok(BRANCH_W)] * N_BRANCH,
        out_shape=[jax.ShapeDtypeStruct((BATCH, CTX_LEN, BRANCH_W), BF16)] * N_BRANCH,
        compiler_params=_params(1),
        name="ctx_attention",
    )(sink, *proj_ctx, lam, sub)


def _merge_kernel(x_ref, mod_ref, g_ref, oa_ref, ob_ref, oc_ref, od_ref, wg_ref, bg_ref, wb_ref, wo_ref, out_ref):
    x = x_ref[0]
    mod = mod_ref[0]
    h = _adaln(x, g_ref[...], mod[0:1], mod[1:2]).astype(BF16)
    y = None
    for i, o_ref in enumerate((oa_ref, ob_ref, oc_ref, od_ref)):
        cols = slice(D_MODEL * i, D_MODEL * (i + 1))
        gate = jax.nn.sigmoid(_dot(h, wg_ref[:, cols]) + bg_ref[:, cols])
        term = gate * _dot(o_ref[0], wb_ref[BRANCH_W * i:BRANCH_W * (i + 1), :])
        y = term if y is None else y + term
    out_ref[0] = x + mod[2:3] * _dot(y.astype(BF16), wo_ref[...])


def _mlp_kernel(x_ref, mod_ref, g_ref, w1_ref, w2_ref, fn_ref, out_ref, *, final):
    x = x_ref[0]
    mod = mod_ref[0]
    h = _adaln(x, g_ref[...], mod[3:4], mod[4:5]).astype(BF16)
    acc = None
    for i in range(FFN_HIDDEN // D_MODEL):
        cols = slice(D_MODEL * i, D_MODEL * (i + 1))
        u = jnp.square(jnp.maximum(_dot(h, w1_ref[:, cols]), 0.0)).astype(BF16)
        part = _dot(u, w2_ref[cols, :])
        acc = part if acc is None else acc + part
    out = x + mod[5:6] * acc
    out_ref[0] = _rms(out, fn_ref[...]) if final else out


def _token_tile(xs, is_ctx):
    tm = CTX_LEN if is_ctx else TM_LAT
    tok = lambda width: pl.BlockSpec((1, tm, width), lambda j, b: (b, j, 0))
    return xs.shape[1] // tm, tok


def _merge(xs, mods, g, outs, w_gate, b_gate, w_branch, w_out, is_ctx):
    n_tiles, tok = _token_tile(xs, is_ctx)
    return pl.pallas_call(
        _merge_kernel,
        grid=(n_tiles, BATCH),
        in_specs=[tok(D_MODEL), _mod_spec(is_ctx), _const_spec((1, D_MODEL))] + [tok(BRANCH_W)] * N_BRANCH + [
            _const_spec((D_MODEL, N_BRANCH * D_MODEL)),
            _const_spec((1, N_BRANCH * D_MODEL)),
            _const_spec((N_BRANCH * BRANCH_W, D_MODEL)),
            _const_spec((D_MODEL, D_MODEL)),
        ],
        out_specs=tok(D_MODEL),
        out_shape=jax.ShapeDtypeStruct(xs.shape, F32),
        compiler_params=_params(2),
        name="merge_ctx" if is_ctx else "merge",
    )(xs, mods, g, *outs, w_gate, b_gate, w_branch, w_out)


def _mlp(xs, mods, g, w1, w2, final_norm, is_ctx, final):
    n_tiles, tok = _token_tile(xs, is_ctx)
    return pl.pallas_call(
        functools.partial(_mlp_kernel, final=final),
        grid=(n_tiles, BATCH),
        in_specs=[tok(D_MODEL), _mod_spec(is_ctx), _const_spec((1, D_MODEL)),
                  _const_spec((D_MODEL, FFN_HIDDEN)), _const_spec((FFN_HIDDEN, D_MODEL)),
                  _const_spec((1, D_MODEL))],
        out_specs=tok(D_MODEL),
        out_shape=jax.ShapeDtypeStruct(xs.shape, F32),
        compiler_params=_params(2),
        name="mlp_ctx" if is_ctx else ("mlp_final" if final else "mlp"),
    )(xs, mods, g, w1, w2, final_norm)


def _rope_tables():
    t = jnp.arange(SEQ, dtype=jnp.int32)
    row = (t // GRID_W).astype(F32)[:, None]
    col = (t % GRID_W).astype(F32)[:, None]

    def table(dim, lane_start, lane_stop):
        quarter = dim // 4
        inv_freq = ROPE_BASE ** (-jnp.arange(quarter, dtype=F32) / quarter)
        ang = jnp.concatenate([row * inv_freq, row * inv_freq, col * inv_freq, col * inv_freq], axis=-1)
        reps = (lane_stop - lane_start) // dim
        cos = jnp.tile(jnp.cos(ang), (1, reps))
        sin = jnp.tile(jnp.sin(ang), (1, reps))
        lower = (np.arange(lane_stop - lane_start) % (2 * quarter)) < quarter
        pad = lambda a, fill: jnp.pad(a, ((0, 0), (lane_start, LANES - lane_stop)), constant_values=fill)
        return [pad(cos, 1.0), pad(jnp.where(lower, -sin, 0.0), 0.0), pad(jnp.where(lower, 0.0, sin), 0.0)]

    tabs = table(MLA_ROPE, MLA_NOPE, MLA_NOPE + MLA_ROPE) + table(DIFF_QK, 0, LANES) + table(WIN_DIM, 0, LANES)
    return jnp.stack(tabs)


def _na_bias_tables(rpb):
    pad = GRID_W - NA_COLS
    rp = jnp.pad(rpb.astype(F32), ((0, 0), (0, 0), (pad, pad)))
    by_col = jnp.stack([rp[:, :, GRID_W - 1 - qc:2 * GRID_W - 1 - qc] for qc in range(GRID_W)], axis=2)
    qc = np.arange(GRID_W)[:, None]
    kc = np.arange(GRID_W)[None, :]
    c_start = np.clip(qc - NA_COLS // 2, 0, GRID_W - NA_COLS)
    in_cols = (kc >= c_start) & (kc < c_start + NA_COLS)
    neg = jnp.full((NA_HEADS, GRID_W, GRID_W), NEG_INF, F32)
    n_strips = GRID_ROWS // NA_STRIP
    tables = []
    for strip in (0, 1, 2, n_strips - 1):
        first_key_row = int(np.clip(strip * NA_STRIP - NA_ROWS // 2, 0, GRID_ROWS - NA_KEY_ROWS))
        rows = []
        for dr in range(NA_STRIP):
            r = strip * NA_STRIP + dr
            r_start = int(np.clip(r - NA_ROWS // 2, 0, GRID_ROWS - NA_ROWS))
            for jw in range(NA_KEY_ROWS):
                key_row = first_key_row + jw
                inside = r_start <= key_row < r_start + NA_ROWS
                rows.append(by_col[:, key_row - r + NA_ROWS - 1] if inside else neg)
        tab = jnp.stack(rows, axis=1).reshape(NA_HEADS, NA_STRIP, NA_KEY_ROWS, GRID_W, GRID_W)
        tab = jnp.where(in_cols[None, None, None], tab, NEG_INF)
        tables.append(jnp.transpose(tab, (0, 1, 3, 2, 4)).reshape(NA_HEADS, TQ, NA_N_LOC))
    return jnp.stack(tables) * LOG2E


def _pad_heads(w, heads, width, keep):
    w = w.reshape(w.shape[0], heads, width)[:, :, keep[0]:keep[1]]
    return jnp.pad(w, ((0, 0), (0, 0), (0, LANES - (keep[1] - keep[0])))).reshape(w.shape[0], heads * LANES)


_WIN_HEAD_ORDER = (0, 2, 1, 3)


def _prep_w_in(w):
    seg = lambda a, b: w[:, a:b]
    kr = jnp.pad(seg(384, 416), ((0, 0), (MLA_NOPE, LANES - MLA_NOPE - MLA_ROPE)))
    wq = seg(1184, 1440).reshape(D_MODEL, WIN_HEADS, WIN_DIM)[:, np.array(_WIN_HEAD_ORDER)].reshape(D_MODEL, 256)
    return jnp.concatenate([seg(0, 384), kr, seg(416, 1184), wq, seg(1440, 2464)], axis=1).astype(BF16)


def _prep_w_branch(w):
    win = w[512:768].reshape(WIN_HEADS, WIN_DIM, D_MODEL)[np.array(_WIN_HEAD_ORDER)].reshape(256, D_MODEL)
    return jnp.concatenate([w[:512], win, w[768:]], axis=0).astype(BF16)


def _lambda_init(layer):
    return 0.8 - 0.6 * math.exp(-0.3 * layer)


def kernel(x, c, ctx, c_ctx, w_ada, b_ada, norm_mix, norm_ffn, w_in, mla_q_norm, mla_kv_norm, w_uq, w_ukv,
           diff_lambda, diff_subln, win_sink, na_rpb, w_gate, b_gate, w_branch, w_out, w_ff1, w_ff2, final_norm):
    cvec = jnp.concatenate([c, c_ctx[None, :], jnp.zeros((N_MOD_ROWS - BATCH - 1, D_MODEL), F32)], axis=0)
    mods = _modulation(cvec, w_ada, b_ada)
    rope_tab = _rope_tables()
    row = lambda v: v.reshape(1, -1)
    xc = ctx
    for l in range(DEPTH):
        need_ctx = l < DEPTH - 1
        lam_init = _lambda_init(l)
        ukv = w_ukv[l].reshape(MLA_KV_RANK, MLA_HEADS, MLA_NOPE + MLA_V)
        proj_w = (row(norm_mix[l]), _prep_w_in(w_in[l]), row(mla_q_norm[l]), row(mla_kv_norm[l]),
                  _pad_heads(w_uq[l], MLA_HEADS, MLA_NOPE + MLA_ROPE, (0, MLA_NOPE + MLA_ROPE)).astype(BF16),
                  _pad_heads(w_ukv[l], MLA_HEADS, MLA_NOPE + MLA_V, (0, MLA_NOPE)).astype(BF16),
                  ukv[:, :, MLA_NOPE:].reshape(MLA_KV_RANK, MLA_HEADS * MLA_V).astype(BF16))
        mq, mk, mv, dq, dk, dv, wq, wk, wv, nq, nk, nv = _project(x, mods[l], *proj_w, rope_tab, is_ctx=False)
        pc = _project(xc, mods[l], *proj_w, None, is_ctx=True)
        sub = jnp.tile(diff_subln[l], 2).reshape(1, LANES)
        lam_specs = [_const_spec((4, DIFF_QK)), _const_spec((1, LANES))]
        outs = (
            _latent_attention(_mla_kernel, mq, mk, mv, pc[1], pc[2], "mla_attention"),
            _latent_attention(functools.partial(_diff_kernel, lam_init=lam_init), dq, dk, dv, pc[4], pc[5],
                              "diff_attention", post=(diff_lambda[l], sub), post_specs=lam_specs),
            _latent_attention(_win_kernel, wq, wk, wv, pc[7], pc[8], "window_attention",
                              pre=(win_sink[l],), pre_specs=[pl.BlockSpec(memory_space=pltpu.SMEM)]),
            _latent_attention(_na_kernel, nq, nk, nv, pc[10], pc[11], "neighbourhood_attention",
                              post=(_na_bias_tables(na_rpb[l]),),
                              post_specs=[_const_spec((4, NA_HEADS, TQ, NA_N_LOC))]),
        )
        merge_w = (w_gate[l].astype(BF16), row(b_gate[l]), _prep_w_branch(w_branch[l]), w_out[l].astype(BF16))
        mlp_w = (row(norm_ffn[l]), w_ff1[l].astype(BF16), w_ff2[l].astype(BF16), row(final_norm))
        x = _merge(x, mods[l], row(norm_mix[l]), outs, *merge_w, is_ctx=False)
        x = _mlp(x, mods[l], *mlp_w, is_ctx=False, final=not need_ctx)
        if need_ctx:
            outs_c = _ctx_attention(pc, win_sink[l], diff_lambda[l], sub, lam_init)
            xc = _merge(xc, mods[l], row(norm_mix[l]), outs_c, *merge_w, is_ctx=True)
            xc = _mlp(xc, mods[l], *mlp_w, is_ctx=True, final=False)
    return x
```

```python
import functools
import math

import numpy as np
import jax
import jax.numpy as jnp
from jax import lax
from jax.experimental import pallas as pl
from jax.experimental.pallas import tpu as pltpu

D_MODEL = 1024
BATCH = 8
SEQ = 2048
DEPTH = 2
CTX_LEN = 256
GRID_W = 64
GRID_ROWS = SEQ // GRID_W
ROPE_BASE = 10000.0
NEG_INF = -1e30
EPS = 1e-6
N_ADA = 6
LOG2E = math.log2(math.e)

MLA_HEADS = 4
MLA_NOPE = 64
MLA_ROPE = 32
MLA_V = 64
MLA_Q_RANK = 256
MLA_KV_RANK = 128
MLA_SCALE = (MLA_NOPE + MLA_ROPE) ** -0.5
DIFF_HEADS = 4
DIFF_QK = 32
DIFF_V = 64
DIFF_SCALE = DIFF_QK ** -0.5
WIN_HEADS = 4
WIN_KV_HEADS = 2
WIN_DIM = 64
WINDOW = 128
WIN_SCALE = WIN_DIM ** -0.5
NA_HEADS = 4
NA_DIM = 64
NA_ROWS = 8
NA_COLS = 16
NA_SCALE = NA_DIM ** -0.5
FFN_HIDDEN = 4 * D_MODEL
N_BRANCH = 4
BRANCH_W = 256

LANES = 128
HALF = LANES // 2
VMEM_LIMIT = 56 * 1024 * 1024

BF16 = jnp.bfloat16
F32 = jnp.float32

P_CQ, P_CKV, P_KR = 0, 256, 384
P_DQ, P_DK, P_DV = 512, 768, 1024
P_WQ, P_WK, P_WV = 1280, 1536, 1664
P_NQ, P_NK, P_NV = 1792, 2048, 2304
P_WIDTH = 2560

TM_PROJ = 512
TQ = 256
TQ_DENSE = 512
TM_LAT = 512
N_MOD_ROWS = 16
CTX_MOD_ROW = BATCH

NA_STRIP = TQ // GRID_W
NA_KEY_ROWS = 12
NA_N_LOC = NA_KEY_ROWS * GRID_W
NA_BIAS_VARIANTS = 3
WIN_N_LOC = TQ + 2 * WINDOW


def _const_spec(shape):
    nd = len(shape)
    return pl.BlockSpec(shape, lambda *_: (0,) * nd, pipeline_mode=pl.Buffered(1))


def _params(n_axes):
    return pltpu.CompilerParams(dimension_semantics=("arbitrary",) * n_axes, vmem_limit_bytes=VMEM_LIMIT)


def _rms(x, g):
    return x * lax.rsqrt(jnp.mean(x * x, axis=-1, keepdims=True) + EPS) * g


def _adaln(x, g, shift, scale):
    return _rms(x, g) * (1.0 + scale) + shift


def _dot(a, b):
    return jnp.dot(a, b, preferred_element_type=F32)


def _dot_nt(a, b):
    return lax.dot_general(a, b, (((1,), (1,)), ((), ())), preferred_element_type=F32)


def _lane_iota(rows):
    return lax.broadcasted_iota(jnp.int32, (rows, LANES), 1)


def _blk(i):
    return slice(LANES * i, LANES * (i + 1))


def _mod_kernel(c_ref, w_ref, b_ref, o_ref):
    c = c_ref[...]
    act = c * jax.nn.sigmoid(c)
    o_ref[0] = jnp.dot(act, w_ref[0], precision=lax.Precision.HIGHEST, preferred_element_type=F32) + b_ref[0]


def _modulation(cvec, w_ada, b_ada):
    tn = 1536
    n_ada = N_ADA * D_MODEL
    out = pl.pallas_call(
        _mod_kernel,
        grid=(DEPTH, n_ada // tn),
        in_specs=[
            pl.BlockSpec((N_MOD_ROWS, D_MODEL), lambda l, n: (0, 0)),
            pl.BlockSpec((1, D_MODEL, tn), lambda l, n: (l, 0, n)),
            pl.BlockSpec((1, 1, tn), lambda l, n: (l, 0, n)),
        ],
        out_specs=pl.BlockSpec((1, N_MOD_ROWS, tn), lambda l, n: (l, 0, n)),
        out_shape=jax.ShapeDtypeStruct((DEPTH, N_MOD_ROWS, n_ada), F32),
        compiler_params=_params(2),
        name="modulation",
    )(cvec, w_ada, b_ada.reshape(DEPTH, 1, n_ada))
    return out.reshape(DEPTH, N_MOD_ROWS, N_ADA, D_MODEL)


def _mod_spec(is_ctx):
    if is_ctx:
        return pl.BlockSpec((1, N_ADA, D_MODEL), lambda j, b: (CTX_MOD_ROW, 0, 0))
    return pl.BlockSpec((1, N_ADA, D_MODEL), lambda j, b: (b, 0, 0))


def _proj_kernel(x_ref, mod_ref, g_ref, win_ref, qn_ref, kvn_ref, wuq_ref, wuk_ref, wuv_ref, *rest, rotary):
    if rotary:
        rope_ref, rest = rest[0], rest[1:]
    mq_ref, mk_ref, mv_ref, dq_ref, dk_ref, dv_ref, wq_ref, wk_ref, wv_ref, nq_ref, nk_ref, nv_ref = rest
    mod = mod_ref[0]
    h = _adaln(x_ref[0], g_ref[...], mod[0:1], mod[1:2]).astype(BF16)
    lane = _lane_iota(h.shape[0])

    def seg(start, width):
        return _dot(h, win_ref[:, start:start + width])

    def rope(x, base, quarter):
        if not rotary:
            return x
        return (x * rope_ref[base]
                + pltpu.roll(x, LANES - quarter, 1) * rope_ref[base + 1]
                + pltpu.roll(x, quarter, 1) * rope_ref[base + 2])

    def put(ref, val, rope_base=None, quarter=None, scale=None):
        for i in range(val.shape[1] // LANES):
            blk = val[:, _blk(i)]
            if rope_base is not None:
                blk = rope(blk, rope_base, quarter)
            if scale is not None:
                blk = blk * scale
            ref[0, :, _blk(i)] = blk.astype(BF16)

    def put_values(ref, val):
        for hd in range(val.shape[1] // HALF):
            blk = val[:, _blk(hd // 2)]
            if hd % 2:
                blk = pltpu.roll(blk, HALF, 1)
            ref[0, :, _blk(hd)] = jnp.where(lane < HALF, blk, 1.0).astype(BF16)

    cq = _rms(seg(P_CQ, MLA_Q_RANK), qn_ref[...]).astype(BF16)
    put(mq_ref, _dot(cq, wuq_ref[...]), 0, MLA_ROPE // 4, MLA_SCALE * LOG2E)
    ckv = _rms(seg(P_CKV, MLA_KV_RANK), kvn_ref[...]).astype(BF16)
    kr = rope(seg(P_KR, LANES), 0, MLA_ROPE // 4)
    kn = _dot(ckv, wuk_ref[...])
    for i in range(MLA_HEADS):
        mk_ref[0, :, _blk(i)] = (kn[:, _blk(i)] + kr).astype(BF16)
    put_values(mv_ref, _dot(ckv, wuv_ref[...]))
    put(dq_ref, seg(P_DQ, 256), 3, DIFF_QK // 4, DIFF_SCALE * LOG2E)
    put(dk_ref, seg(P_DK, 256), 3, DIFF_QK // 4)
    put_values(dv_ref, seg(P_DV, 256))
    put(wq_ref, seg(P_WQ, 256), 6, WIN_DIM // 4, WIN_SCALE * LOG2E)
    put(wk_ref, seg(P_WK, 128), 6, WIN_DIM // 4)
    put_values(wv_ref, seg(P_WV, 128))
    put(nq_ref, seg(P_NQ, 256), scale=NA_SCALE * LOG2E)
    put(nk_ref, seg(P_NK, 256))
    put_values(nv_ref, seg(P_NV, 256))


_PROJ_OUT_WIDTHS = (512, 512, 512, 256, 256, 512, 256, 128, 256, 256, 256, 512)


def _project(xs, mods, g, w_in_p, q_norm, kv_norm, w_uq_p, w_uk_p, w_uv, rope_tab, is_ctx):
    n_tok = xs.shape[1]
    tm = min(TM_PROJ, n_tok)
    tok = lambda width: pl.BlockSpec((1, tm, width), lambda j, b: (b, j, 0))
    in_specs = [
        tok(D_MODEL),
        _mod_spec(is_ctx),
        _const_spec((1, D_MODEL)),
        _const_spec((D_MODEL, P_WIDTH)),
        _const_spec((1, MLA_Q_RANK)),
        _const_spec((1, MLA_KV_RANK)),
        _const_spec((MLA_Q_RANK, MLA_HEADS * LANES)),
        _const_spec((MLA_KV_RANK, MLA_HEADS * LANES)),
        _const_spec((MLA_KV_RANK, MLA_HEADS * MLA_V)),
    ]
    args = [xs, mods, g, w_in_p, q_norm, kv_norm, w_uq_p, w_uk_p, w_uv]
    if not is_ctx:
        in_specs.append(pl.BlockSpec((9, tm, LANES), lambda j, b: (0, j, 0)))
        args.append(rope_tab)
    return pl.pallas_call(
        functools.partial(_proj_kernel, rotary=not is_ctx),
        grid=(n_tok // tm, BATCH),
        in_specs=in_specs,
        out_specs=[tok(w) for w in _PROJ_OUT_WIDTHS],
        out_shape=[jax.ShapeDtypeStruct((BATCH, n_tok, w), BF16) for w in _PROJ_OUT_WIDTHS],
        compiler_params=_params(2),
        name="project_ctx" if is_ctx else "project",
    )(*args)


def _attend(q, ks, vs, lane, post=None, sink=None):
    ss = []
    for i, k in enumerate(ks):
        s = _dot_nt(q, k)
        if post is not None and post[i] is not None:
            s = post[i](s)
        ss.append(s)
    m = ss[0].max(axis=-1, keepdims=True)
    for s in ss[1:]:
        m = jnp.maximum(m, s.max(axis=-1, keepdims=True))
    if sink is not None:
        m = jnp.maximum(m, sink)
    acc = None
    for s, v in zip(ss, vs):
        part = _dot(jnp.exp2(s - m).astype(BF16), v)
        acc = part if acc is None else acc + part
    if sink is not None:
        acc = acc + jnp.where(lane >= HALF, jnp.exp2(sink - m), 0.0)
    return acc


def _normalise(acc):
    return acc * (1.0 / pltpu.roll(acc, HALF, 1))


def _pair(lane, even, odd):
    return jnp.where(lane < HALF, even, pltpu.roll(odd, HALF, 1))


def _half_mask(lane, q, start, width):
    return jnp.where((lane >= start) & (lane < start + width), q, jnp.zeros_like(q))


def _mla_heads(q_ref, kv_parts, o_ref, lane):
    for blk in range(MLA_HEADS // 2):
        outs = []
        for hd in (2 * blk, 2 * blk + 1):
            acc = _attend(q_ref[0, :, _blk(hd)], [k[0, :, _blk(hd)] for k, _ in kv_parts],
                          [v[0, :, _blk(hd)] for _, v in kv_parts], lane)
            outs.append(_normalise(acc))
        o_ref[0, :, _blk(blk)] = _pair(lane, *outs).astype(BF16)


def _diff_lambda(lam_ref, lam_init):
    lp = lam_ref[...]
    return (jnp.exp(jnp.sum(lp[0:1] * lp[1:2], axis=-1, keepdims=True))
            - jnp.exp(jnp.sum(lp[2:3] * lp[3:4], axis=-1, keepdims=True)) + lam_init)


def _diff_heads(q_ref, kv_parts, lam_ref, sub_ref, o_ref, lane, lam_init):
    lam = _diff_lambda(lam_ref, lam_init)
    for blk in range(DIFF_HEADS // 2):
        qb = q_ref[0, :, _blk(blk)]
        ks = [k[0, :, _blk(blk)] for k, _ in kv_parts]
        outs = []
        for u in range(2):
            vs = [v[0, :, _blk(2 * blk + u)] for _, v in kv_parts]
            a1 = _attend(_half_mask(lane, qb, HALF * u, DIFF_QK), ks, vs, lane)
            a2 = _attend(_half_mask(lane, qb, HALF * u + DIFF_QK, DIFF_QK), ks, vs, lane)
            outs.append(_normalise(a1) - lam * _normalise(a2))
        pair = _pair(lane, *outs)
        sq = pair * pair
        lo = jnp.sum(jnp.where(lane < HALF, sq, 0.0), axis=-1, keepdims=True)
        hi = jnp.sum(jnp.where(lane < HALF, 0.0, sq), axis=-1, keepdims=True)
        ms = jnp.where(lane < HALF, lo, hi) * (1.0 / DIFF_V)
        o_ref[0, :, _blk(blk)] = (pair * lax.rsqrt(ms + EPS) * sub_ref[...] * (1.0 - lam_init)).astype(BF16)


def _win_heads(q_ref, k_parts, v_parts, posts, sink_ref, o_ref, lane):
    for blk in range(2):
        qb = q_ref[0, :, _blk(blk)]
        outs = []
        for u in range(WIN_KV_HEADS):
            acc = _attend(_half_mask(lane, qb, HALF * u, HALF), k_parts, [v[:, _blk(u)] for v in v_parts], lane,
                          post=posts, sink=sink_ref[blk + 2 * u] * LOG2E)
            outs.append(_normalise(acc))
        o_ref[0, :, _blk(blk)] = _pair(lane, *outs).astype(BF16)


def _na_heads(q_ref, kv_parts, posts_of_head, o_ref, lane):
    for blk in range(NA_HEADS // 2):
        qb = q_ref[0, :, _blk(blk)]
        outs = []
        for u in range(2):
            hd = 2 * blk + u
            acc = _attend(_half_mask(lane, qb, HALF * u, HALF), [k(_blk(blk)) for k, _ in kv_parts],
                          [v(_blk(hd)) for _, v in kv_parts], lane, post=posts_of_head(hd))
            outs.append(_normalise(acc))
        o_ref[0, :, _blk(blk)] = _pair(lane, *outs).astype(BF16)


def _mla_kernel(q_ref, kl_ref, vl_ref, kc_ref, vc_ref, o_ref):
    _mla_heads(q_ref, [(kl_ref, vl_ref), (kc_ref, vc_ref)], o_ref, _lane_iota(TQ_DENSE))


def _diff_kernel(q_ref, kl_ref, vl_ref, kc_ref, vc_ref, lam_ref, sub_ref, o_ref, *, lam_init):
    _diff_heads(q_ref, [(kl_ref, vl_ref), (kc_ref, vc_ref)], lam_ref, sub_ref, o_ref, _lane_iota(TQ_DENSE),
                lam_init)

def _win_kernel(sink_ref, q_ref, kl_ref, vl_ref, kc_ref, vc_ref, o_ref):
    j = pl.program_id(1)
    start = pl.multiple_of(jnp.clip(j * TQ - WINDOW, 0, SEQ - WIN_N_LOC), WINDOW)
    qpos = j * TQ + lax.broadcasted_iota(jnp.int32, (TQ, WIN_N_LOC), 0)
    kpos = start + lax.broadcasted_iota(jnp.int32, (TQ, WIN_N_LOC), 1)
    allowed = jnp.abs(qpos - kpos) <= WINDOW
    band = lambda s: jnp.where(allowed, s, NEG_INF)
    _win_heads(q_ref, [kl_ref[0, pl.ds(start, WIN_N_LOC), :], kc_ref[0]],
               [vl_ref[0, pl.ds(start, WIN_N_LOC), :], vc_ref[0]], [band, None], sink_ref, o_ref, _lane_iota(TQ))


def _na_kernel(q_ref, kl_ref, vl_ref, kc_ref, vc_ref, bias_ref, o_ref):
    j = pl.program_id(1)
    n_strips = GRID_ROWS // NA_STRIP
    first = jnp.clip(j * NA_STRIP - NA_ROWS // 2, 0, GRID_ROWS - NA_KEY_ROWS)
    start = pl.multiple_of(first * GRID_W, GRID_W)
    variant = (j >= 1).astype(jnp.int32) + (j >= n_strips - 1).astype(jnp.int32)
    parts = [(lambda ls: kl_ref[0, pl.ds(start, NA_N_LOC), ls], lambda ls: vl_ref[0, pl.ds(start, NA_N_LOC), ls]),
             (lambda ls: kc_ref[0, :, ls], lambda ls: vc_ref[0, :, ls])]
    posts = lambda hd: [lambda s: s + bias_ref[variant, hd], None]
    _na_heads(q_ref, parts, posts, o_ref, _lane_iota(TQ))


def _local_kernel(sink_ref, wq, wkl, wvl, wkc, wvc, nq, nkl, nvl, nkc, nvc, bias_ref, oc_ref, od_ref):
    _win_kernel(sink_ref, wq, wkl, wvl, wkc, wvc, oc_ref)
    _na_kernel(nq, nkl, nvl, nkc, nvc, bias_ref, od_ref)


def _local_attention(sink, win, win_ctx, na, na_ctx, bias):
    def specs(q, kl, vl):
        qw, kw, vw = q.shape[-1], kl.shape[-1], vl.shape[-1]
        return [pl.BlockSpec((1, TQ, qw), lambda b, j: (b, j, 0)),
                pl.BlockSpec((1, SEQ, kw), lambda b, j: (b, 0, 0)),
                pl.BlockSpec((1, SEQ, vw), lambda b, j: (b, 0, 0)),
                pl.BlockSpec((1, CTX_LEN, kw), lambda b, j: (b, 0, 0)),
                pl.BlockSpec((1, CTX_LEN, vw), lambda b, j: (b, 0, 0))]
    out_spec = pl.BlockSpec((1, TQ, BRANCH_W), lambda b, j: (b, j, 0))
    return pl.pallas_call(
        _local_kernel,
        grid=(BATCH, SEQ // TQ),
        in_specs=[pl.BlockSpec(memory_space=pltpu.SMEM)] + specs(*win) + specs(*na) + [_const_spec(bias.shape)],
        out_specs=[out_spec, out_spec],
        out_shape=[jax.ShapeDtypeStruct((BATCH, SEQ, BRANCH_W), BF16)] * 2,
        compiler_params=_params(2),
        name="local_attention",
    )(sink, *win, *win_ctx, *na, *na_ctx, bias)


def _latent_attention(body, tq, q, kl, vl, kc, vc, name, pre=(), pre_specs=(), post=(), post_specs=()):
    qw, kw, vw = q.shape[-1], kl.shape[-1], vl.shape[-1]
    return pl.pallas_call(
        body,
        grid=(BATCH, SEQ // tq),
        in_specs=list(pre_specs) + [
            pl.BlockSpec((1, tq, qw), lambda b, j: (b, j, 0)),
            pl.BlockSpec((1, SEQ, kw), lambda b, j: (b, 0, 0)),
            pl.BlockSpec((1, SEQ, vw), lambda b, j: (b, 0, 0)),
            pl.BlockSpec((1, CTX_LEN, kw), lambda b, j: (b, 0, 0)),
            pl.BlockSpec((1, CTX_LEN, vw), lambda b, j: (b, 0, 0)),
        ] + list(post_specs),
        out_specs=pl.BlockSpec((1, tq, BRANCH_W), lambda b, j: (b, j, 0)),
        out_shape=jax.ShapeDtypeStruct((BATCH, SEQ, BRANCH_W), BF16),
        compiler_params=_params(2),
        name=name,
    )(*pre, q, kl, vl, kc, vc, *post)


def _ctx_kernel(sink_ref, mq, mk, mv, dq, dk, dv, wq, wk, wv, nq, nk, nv, lam_ref, sub_ref,
                oa_ref, ob_ref, oc_ref, od_ref, *, lam_init):
    lane = _lane_iota(CTX_LEN)
    _mla_heads(mq, [(mk, mv)], oa_ref, lane)
    _diff_heads(dq, [(dk, dv)], lam_ref, sub_ref, ob_ref, lane, lam_init)
    _win_heads(wq, [wk[0]], [wv[0]], None, sink_ref, oc_ref, lane)
    _na_heads(nq, [(lambda ls: nk[0, :, ls], lambda ls: nv[0, :, ls])], lambda hd: None, od_ref, lane)


def _ctx_attention(proj_ctx, sink, lam, sub, lam_init):
    tok = lambda width: pl.BlockSpec((1, CTX_LEN, width), lambda b: (b, 0, 0))
    return pl.pallas_call(
        functools.partial(_ctx_kernel, lam_init=lam_init),
        grid=(BATCH,),
        in_specs=[pl.BlockSpec(memory_space=pltpu.SMEM)] + [tok(w) for w in _PROJ_OUT_WIDTHS]
        + [_const_spec((4, DIFF_QK)), _const_spec((1, LANES))],
        out_specs=[tok(BRANCH_W)] * N_BRANCH,
        out_shape=[jax.ShapeDtypeStruct((BATCH, CTX_LEN, BRANCH_W), BF16)] * N_BRANCH,
        compiler_params=_params(1),
        name="ctx_attention",
    )(sink, *proj_ctx, lam, sub)


def _merge_kernel(x_ref, mod_ref, g_ref, oa_ref, ob_ref, oc_ref, od_ref, wg_ref, bg_ref, wb_ref, wo_ref, out_ref):
    x = x_ref[0]
    mod = mod_ref[0]
    h = _adaln(x, g_ref[...], mod[0:1], mod[1:2]).astype(BF16)
    y = None
    for i, o_ref in enumerate((oa_ref, ob_ref, oc_ref, od_ref)):
        cols = slice(D_MODEL * i, D_MODEL * (i + 1))
        gate = jax.nn.sigmoid(_dot(h, wg_ref[:, cols]) + bg_ref[:, cols])
        term = gate * _dot(o_ref[0], wb_ref[BRANCH_W * i:BRANCH_W * (i + 1), :])
        y = term if y is None else y + term
    out_ref[0] = x + mod[2:3] * _dot(y.astype(BF16), wo_ref[...])


def _mlp_kernel(x_ref, mod_ref, g_ref, w1_ref, w2_ref, fn_ref, out_ref, *, final):
    x = x_ref[0]
    mod = mod_ref[0]
    h = _adaln(x, g_ref[...], mod[3:4], mod[4:5]).astype(BF16)
    acc = None
    for i in range(FFN_HIDDEN // D_MODEL):
        cols = slice(D_MODEL * i, D_MODEL * (i + 1))
        u = jnp.square(jnp.maximum(_dot(h, w1_ref[:, cols]), 0.0)).astype(BF16)
        part = _dot(u, w2_ref[cols, :])
        acc = part if acc is None else acc + part
    out = x + mod[5:6] * acc
    out_ref[0] = _rms(out, fn_ref[...]) if final else out


def _token_tile(xs, is_ctx):
    tm = CTX_LEN if is_ctx else TM_LAT
    tok = lambda width: pl.BlockSpec((1, tm, width), lambda j, b: (b, j, 0))
    return xs.shape[1] // tm, tok


def _merge(xs, mods, g, outs, w_gate, b_gate, w_branch, w_out, is_ctx):
    n_tiles, tok = _token_tile(xs, is_ctx)
    return pl.pallas_call(
        _merge_kernel,
        grid=(n_tiles, BATCH),
        in_specs=[tok(D_MODEL), _mod_spec(is_ctx), _const_spec((1, D_MODEL))] + [tok(BRANCH_W)] * N_BRANCH + [
            _const_spec((D_MODEL, N_BRANCH * D_MODEL)),
            _const_spec((1, N_BRANCH * D_MODEL)),
            _const_spec((N_BRANCH * BRANCH_W, D_MODEL)),
            _const_spec((D_MODEL, D_MODEL)),
        ],
        out_specs=tok(D_MODEL),
        out_shape=jax.ShapeDtypeStruct(xs.shape, F32),
        compiler_params=_params(2),
        name="merge_ctx" if is_ctx else "merge",
    )(xs, mods, g, *outs, w_gate, b_gate, w_branch, w_out)


def _mlp(xs, mods, g, w1, w2, final_norm, is_ctx, final):
    n_tiles, tok = _token_tile(xs, is_ctx)
    return pl.pallas_call(
        functools.partial(_mlp_kernel, final=final),
        grid=(n_tiles, BATCH),
        in_specs=[tok(D_MODEL), _mod_spec(is_ctx), _const_spec((1, D_MODEL)),
                  _const_spec((D_MODEL, FFN_HIDDEN)), _const_spec((FFN_HIDDEN, D_MODEL)),
                  _const_spec((1, D_MODEL))],
        out_specs=tok(D_MODEL),
        out_shape=jax.ShapeDtypeStruct(xs.shape, F32),
        compiler_params=_params(2),
        name="mlp_ctx" if is_ctx else ("mlp_final" if final else "mlp"),
    )(xs, mods, g, w1, w2, final_norm)


def _rope_tables():
    t = jnp.arange(SEQ, dtype=jnp.int32)
    row = (t // GRID_W).astype(F32)[:, None]
    col = (t % GRID_W).astype(F32)[:, None]

    def table(dim, lane_start, lane_stop):
        quarter = dim // 4
        inv_freq = ROPE_BASE ** (-jnp.arange(quarter, dtype=F32) / quarter)
        ang = jnp.concatenate([row * inv_freq, row * inv_freq, col * inv_freq, col * inv_freq], axis=-1)
        reps = (lane_stop - lane_start) // dim
        cos = jnp.tile(jnp.cos(ang), (1, reps))
        sin = jnp.tile(jnp.sin(ang), (1, reps))
        lower = (np.arange(lane_stop - lane_start) % (2 * quarter)) < quarter
        pad = lambda a, fill: jnp.pad(a, ((0, 0), (lane_start, LANES - lane_stop)), constant_values=fill)
        return [pad(cos, 1.0), pad(jnp.where(lower, -sin, 0.0), 0.0), pad(jnp.where(lower, 0.0, sin), 0.0)]

    tabs = table(MLA_ROPE, MLA_NOPE, MLA_NOPE + MLA_ROPE) + table(DIFF_QK, 0, LANES) + table(WIN_DIM, 0, LANES)
    return jnp.stack(tabs)


def _na_bias_tables(rpb):
    n_dr = 2 * NA_ROWS - 1
    period = 2 * GRID_W - 1
    pad = GRID_W - NA_COLS
    rp = jnp.pad(rpb.astype(F32), ((0, 0), (0, 0), (pad, pad)))
    flat = jnp.tile(rp, (1, 1, GRID_W + 1))[:, :, :GRID_W * (period + 1)]
    by_col = flat.reshape(NA_HEADS, n_dr, GRID_W, period + 1)[:, :, ::-1, :GRID_W]
    by_query = jnp.pad(jnp.transpose(by_col, (0, 2, 1, 3)), ((0, 0), (0, 0), (NA_STRIP, NA_STRIP), (0, 0)))
    qc = np.arange(GRID_W)[:, None]
    kc = np.arange(GRID_W)[None, :]
    c_start = np.clip(qc - NA_COLS // 2, 0, GRID_W - NA_COLS)
    in_cols = (kc >= c_start) & (kc < c_start + NA_COLS)
    n_strips = GRID_ROWS // NA_STRIP
    tables, masks = [], []
    for strip in (0, 1, n_strips - 1):
        first_key_row = int(np.clip(strip * NA_STRIP - NA_ROWS // 2, 0, GRID_ROWS - NA_KEY_ROWS))
        offset = first_key_row - strip * NA_STRIP + NA_ROWS - 1
        tables.append(jnp.stack(
            [by_query[:, :, offset - dr + NA_STRIP:offset - dr + NA_STRIP + NA_KEY_ROWS] for dr in range(NA_STRIP)],
            axis=1).reshape(NA_HEADS, TQ, NA_N_LOC))
        r = strip * NA_STRIP + np.arange(NA_STRIP)[:, None]
        r_start = np.clip(r - NA_ROWS // 2, 0, GRID_ROWS - NA_ROWS)
        key_row = first_key_row + np.arange(NA_KEY_ROWS)[None, :]
        in_rows = (key_row >= r_start) & (key_row < r_start + NA_ROWS)
        masks.append((in_rows[:, None, :, None] & in_cols[None, :, None, :]).reshape(TQ, NA_N_LOC))
    return jnp.where(np.stack(masks)[:, None], jnp.stack(tables), NEG_INF) * LOG2E


def _pad_heads(w, heads, width, keep):
    w = w.reshape(w.shape[0], heads, width)[:, :, keep[0]:keep[1]]
    return jnp.pad(w, ((0, 0), (0, 0), (0, LANES - (keep[1] - keep[0])))).reshape(w.shape[0], heads * LANES)


_WIN_HEAD_ORDER = (0, 2, 1, 3)


def _prep_w_in(w):
    seg = lambda a, b: w[:, a:b]
    kr = jnp.pad(seg(384, 416), ((0, 0), (MLA_NOPE, LANES - MLA_NOPE - MLA_ROPE)))
    wq = seg(1184, 1440).reshape(D_MODEL, WIN_HEADS, WIN_DIM)[:, np.array(_WIN_HEAD_ORDER)].reshape(D_MODEL, 256)
    return jnp.concatenate([seg(0, 384), kr, seg(416, 1184), wq, seg(1440, 2464)], axis=1).astype(BF16)


def _prep_w_branch(w):
    win = w[512:768].reshape(WIN_HEADS, WIN_DIM, D_MODEL)[np.array(_WIN_HEAD_ORDER)].reshape(256, D_MODEL)
    return jnp.concatenate([w[:512], win, w[768:]], axis=0).astype(BF16)


def _lambda_init(layer):
    return 0.8 - 0.6 * math.exp(-0.3 * layer)


def kernel(x, c, ctx, c_ctx, w_ada, b_ada, norm_mix, norm_ffn, w_in, mla_q_norm, mla_kv_norm, w_uq, w_ukv,
           diff_lambda, diff_subln, win_sink, na_rpb, w_gate, b_gate, w_branch, w_out, w_ff1, w_ff2, final_norm):
    cvec = jnp.concatenate([c, c_ctx[None, :], jnp.zeros((N_MOD_ROWS - BATCH - 1, D_MODEL), F32)], axis=0)
    mods = _modulation(cvec, w_ada, b_ada)
    rope_tab = _rope_tables()
    row = lambda v: v.reshape(1, -1)
    xc = ctx
    for l in range(DEPTH):
        need_ctx = l < DEPTH - 1
        lam_init = _lambda_init(l)
        ukv = w_ukv[l].reshape(MLA_KV_RANK, MLA_HEADS, MLA_NOPE + MLA_V)
        proj_w = (row(norm_mix[l]), _prep_w_in(w_in[l]), row(mla_q_norm[l]), row(mla_kv_norm[l]),
                  _pad_heads(w_uq[l], MLA_HEADS, MLA_NOPE + MLA_ROPE, (0, MLA_NOPE + MLA_ROPE)).astype(BF16),
                  _pad_heads(w_ukv[l], MLA_HEADS, MLA_NOPE + MLA_V, (0, MLA_NOPE)).astype(BF16),
                  ukv[:, :, MLA_NOPE:].reshape(MLA_KV_RANK, MLA_HEADS * MLA_V).astype(BF16))
        mq, mk, mv, dq, dk, dv, wq, wk, wv, nq, nk, nv = _project(x, mods[l], *proj_w, rope_tab, is_ctx=False)
        pc = _project(xc, mods[l], *proj_w, None, is_ctx=True)
        sub = jnp.tile(diff_subln[l], 2).reshape(1, LANES)
        lam_specs = [_const_spec((4, DIFF_QK)), _const_spec((1, LANES))]
        oa = _latent_attention(_mla_kernel, TQ_DENSE, mq, mk, mv, pc[1], pc[2], "mla_attention")
        ob = _latent_attention(functools.partial(_diff_kernel, lam_init=lam_init), TQ_DENSE, dq, dk, dv, pc[4], pc[5],
                               "diff_attention", post=(diff_lambda[l], sub), post_specs=lam_specs)
        oc, od = _local_attention(win_sink[l], (wq, wk, wv), (pc[7], pc[8]), (nq, nk, nv), (pc[10], pc[11]),
                                  _na_bias_tables(na_rpb[l]))
        outs = (oa, ob, oc, od)
        merge_w = (w_gate[l].astype(BF16), row(b_gate[l]), _prep_w_branch(w_branch[l]), w_out[l].astype(BF16))
        mlp_w = (row(norm_ffn[l]), w_ff1[l].astype(BF16), w_ff2[l].astype(BF16), row(final_norm))
        x = _merge(x, mods[l], row(norm_mix[l]), outs, *merge_w, is_ctx=False)
        x = _mlp(x, mods[l], *mlp_w, is_ctx=False, final=not need_ctx)
        if need_ctx:
            outs_c = _ctx_attention(pc, win_sink[l], diff_lambda[l], sub, lam_init)
            xc = _merge(xc, mods[l], row(norm_mix[l]), outs_c, *merge_w, is_ctx=True)
            xc = _mlp(xc, mods[l], *mlp_w, is_ctx=True, final=False)
    return x
```

```python
import functools
import math

import numpy as np
import jax
import jax.numpy as jnp
from jax import lax
from jax.experimental import pallas as pl
from jax.experimental.pallas import tpu as pltpu

D_MODEL = 1024
BATCH = 8
SEQ = 2048
DEPTH = 2
CTX_LEN = 256
GRID_W = 64
GRID_ROWS = SEQ // GRID_W
ROPE_BASE = 10000.0
NEG_INF = -1e30
EPS = 1e-6
N_ADA = 6
LOG2E = math.log2(math.e)

MLA_HEADS = 4
MLA_NOPE = 64
MLA_ROPE = 32
MLA_V = 64
MLA_Q_RANK = 256
MLA_KV_RANK = 128
MLA_SCALE = (MLA_NOPE + MLA_ROPE) ** -0.5
DIFF_HEADS = 4
DIFF_QK = 32
DIFF_V = 64
DIFF_SCALE = DIFF_QK ** -0.5
WIN_HEADS = 4
WIN_KV_HEADS = 2
WIN_DIM = 64
WINDOW = 128
WIN_SCALE = WIN_DIM ** -0.5
NA_HEADS = 4
NA_DIM = 64
NA_ROWS = 8
NA_COLS = 16
NA_SCALE = NA_DIM ** -0.5
FFN_HIDDEN = 4 * D_MODEL
N_BRANCH = 4
BRANCH_W = 256

LANES = 128
HALF = LANES // 2
VMEM_LIMIT = 56 * 1024 * 1024

BF16 = jnp.bfloat16
F32 = jnp.float32

P_CQ, P_CKV, P_KR = 0, 256, 384
P_DQ, P_DK, P_DV = 512, 768, 1024
P_WQ, P_WK, P_WV = 1280, 1536, 1664
P_NQ, P_NK, P_NV = 1792, 2048, 2304
P_WIDTH = 2560

TM_PROJ = 512
TQ = 256
TQ_DENSE = 512
TM_LAT = 512
N_MOD_ROWS = 16
CTX_MOD_ROW = BATCH

NA_STRIP = TQ // GRID_W
NA_KEY_ROWS = 12
NA_N_LOC = NA_KEY_ROWS * GRID_W
NA_BIAS_VARIANTS = 3
WIN_N_LOC = TQ + 2 * WINDOW


def _const_spec(shape):
    nd = len(shape)
    return pl.BlockSpec(shape, lambda *_: (0,) * nd, pipeline_mode=pl.Buffered(1))


def _params(n_axes):
    return pltpu.CompilerParams(dimension_semantics=("arbitrary",) * n_axes, vmem_limit_bytes=VMEM_LIMIT)


def _rms(x, g):
    return x * lax.rsqrt(jnp.mean(x * x, axis=-1, keepdims=True) + EPS) * g


def _adaln(x, g, shift, scale):
    return _rms(x, g) * (1.0 + scale) + shift


def _dot(a, b):
    return jnp.dot(a, b, preferred_element_type=F32)


def _dot_nt(a, b):
    return lax.dot_general(a, b, (((1,), (1,)), ((), ())), preferred_element_type=F32)


def _lane_iota(rows):
    return lax.broadcasted_iota(jnp.int32, (rows, LANES), 1)


def _blk(i):
    return slice(LANES * i, LANES * (i + 1))


def _mod_kernel(c_ref, w_ref, b_ref, o_ref):
    c = c_ref[...]
    act = c * jax.nn.sigmoid(c)
    o_ref[0] = jnp.dot(act, w_ref[0], precision=lax.Precision.HIGHEST, preferred_element_type=F32) + b_ref[0]


def _modulation(cvec, w_ada, b_ada):
    tn = 1536
    n_ada = N_ADA * D_MODEL
    out = pl.pallas_call(
        _mod_kernel,
        grid=(DEPTH, n_ada // tn),
        in_specs=[
            pl.BlockSpec((N_MOD_ROWS, D_MODEL), lambda l, n: (0, 0)),
            pl.BlockSpec((1, D_MODEL, tn), lambda l, n: (l, 0, n)),
            pl.BlockSpec((1, 1, tn), lambda l, n: (l, 0, n)),
        ],
        out_specs=pl.BlockSpec((1, N_MOD_ROWS, tn), lambda l, n: (l, 0, n)),
        out_shape=jax.ShapeDtypeStruct((DEPTH, N_MOD_ROWS, n_ada), F32),
        compiler_params=_params(2),
        name="modulation",
    )(cvec, w_ada, b_ada.reshape(DEPTH, 1, n_ada))
    return out.reshape(DEPTH, N_MOD_ROWS, N_ADA, D_MODEL)


def _mod_spec(is_ctx):
    if is_ctx:
        return pl.BlockSpec((1, N_ADA, D_MODEL), lambda j, b: (CTX_MOD_ROW, 0, 0))
    return pl.BlockSpec((1, N_ADA, D_MODEL), lambda j, b: (b, 0, 0))


def _proj_kernel(x_ref, mod_ref, g_ref, win_ref, qn_ref, kvn_ref, wuq_ref, wuk_ref, wuv_ref, *rest, rotary):
    if rotary:
        rope_ref, rest = rest[0], rest[1:]
    mq_ref, mk_ref, mv_ref, dq_ref, dk_ref, dv_ref, wq_ref, wk_ref, wv_ref, nq_ref, nk_ref, nv_ref = rest
    mod = mod_ref[0]
    h = _adaln(x_ref[0], g_ref[...], mod[0:1], mod[1:2]).astype(BF16)
    lane = _lane_iota(h.shape[0])

    def seg(start, width):
        return _dot(h, win_ref[:, start:start + width])

    def rope(x, base, quarter):
        if not rotary:
            return x
        return (x * rope_ref[base]
                + pltpu.roll(x, LANES - quarter, 1) * rope_ref[base + 1]
                + pltpu.roll(x, quarter, 1) * rope_ref[base + 2])

    def put(ref, val, rope_base=None, quarter=None, scale=None):
        for i in range(val.shape[1] // LANES):
            blk = val[:, _blk(i)]
            if rope_base is not None:
                blk = rope(blk, rope_base, quarter)
            if scale is not None:
                blk = blk * scale
            ref[0, :, _blk(i)] = blk.astype(BF16)

    def put_values(ref, val):
        for hd in range(val.shape[1] // HALF):
            blk = val[:, _blk(hd // 2)]
            if hd % 2:
                blk = pltpu.roll(blk, HALF, 1)
            ref[0, :, _blk(hd)] = jnp.where(lane < HALF, blk, 1.0).astype(BF16)

    cq = _rms(seg(P_CQ, MLA_Q_RANK), qn_ref[...]).astype(BF16)
    put(mq_ref, _dot(cq, wuq_ref[...]), 0, MLA_ROPE // 4, MLA_SCALE * LOG2E)
    ckv = _rms(seg(P_CKV, MLA_KV_RANK), kvn_ref[...]).astype(BF16)
    kr = rope(seg(P_KR, LANES), 0, MLA_ROPE // 4)
    kn = _dot(ckv, wuk_ref[...])
    for i in range(MLA_HEADS):
        mk_ref[0, :, _blk(i)] = (kn[:, _blk(i)] + kr).astype(BF16)
    put_values(mv_ref, _dot(ckv, wuv_ref[...]))
    put(dq_ref, seg(P_DQ, 256), 3, DIFF_QK // 4, DIFF_SCALE * LOG2E)
    put(dk_ref, seg(P_DK, 256), 3, DIFF_QK // 4)
    put_values(dv_ref, seg(P_DV, 256))
    put(wq_ref, seg(P_WQ, 256), 6, WIN_DIM // 4, WIN_SCALE * LOG2E)
    put(wk_ref, seg(P_WK, 128), 6, WIN_DIM // 4)
    put_values(wv_ref, seg(P_WV, 128))
    put(nq_ref, seg(P_NQ, 256), scale=NA_SCALE * LOG2E)
    put(nk_ref, seg(P_NK, 256))
    put_values(nv_ref, seg(P_NV, 256))


_PROJ_OUT_WIDTHS = (512, 512, 512, 256, 256, 512, 256, 128, 256, 256, 256, 512)


def _project(xs, mods, g, w_in_p, q_norm, kv_norm, w_uq_p, w_uk_p, w_uv, rope_tab, is_ctx):
    n_tok = xs.shape[1]
    tm = min(TM_PROJ, n_tok)
    tok = lambda width: pl.BlockSpec((1, tm, width), lambda j, b: (b, j, 0))
    in_specs = [
        tok(D_MODEL),
        _mod_spec(is_ctx),
        _const_spec((1, D_MODEL)),
        _const_spec((D_MODEL, P_WIDTH)),
        _const_spec((1, MLA_Q_RANK)),
        _const_spec((1, MLA_KV_RANK)),
        _const_spec((MLA_Q_RANK, MLA_HEADS * LANES)),
        _const_spec((MLA_KV_RANK, MLA_HEADS * LANES)),
        _const_spec((MLA_KV_RANK, MLA_HEADS * MLA_V)),
    ]
    args = [xs, mods, g, w_in_p, q_norm, kv_norm, w_uq_p, w_uk_p, w_uv]
    if not is_ctx:
        in_specs.append(pl.BlockSpec((9, tm, LANES), lambda j, b: (0, j, 0)))
        args.append(rope_tab)
    return pl.pallas_call(
        functools.partial(_proj_kernel, rotary=not is_ctx),
        grid=(n_tok // tm, BATCH),
        in_specs=in_specs,
        out_specs=[tok(w) for w in _PROJ_OUT_WIDTHS],
        out_shape=[jax.ShapeDtypeStruct((BATCH, n_tok, w), BF16) for w in _PROJ_OUT_WIDTHS],
        compiler_params=_params(2),
        name="project_ctx" if is_ctx else "project",
    )(*args)


def _attend(q, ks, vs, lane, post=None, sink=None):
    ss = []
    for i, k in enumerate(ks):
        s = _dot_nt(q, k)
        if post is not None and post[i] is not None:
            s = post[i](s)
        ss.append(s)
    m = ss[0].max(axis=-1, keepdims=True)
    for s in ss[1:]:
        m = jnp.maximum(m, s.max(axis=-1, keepdims=True))
    if sink is not None:
        m = jnp.maximum(m, sink)
    acc = None
    for s, v in zip(ss, vs):
        part = _dot(jnp.exp2(s - m).astype(BF16), v)
        acc = part if acc is None else acc + part
    if sink is not None:
        acc = acc + jnp.where(lane >= HALF, jnp.exp2(sink - m), 0.0)
    return acc


def _normalise(acc):
    return acc * (1.0 / pltpu.roll(acc, HALF, 1))


def _pair(lane, even, odd):
    return jnp.where(lane < HALF, even, pltpu.roll(odd, HALF, 1))


def _half_mask(lane, q, start, width):
    return jnp.where((lane >= start) & (lane < start + width), q, jnp.zeros_like(q))


def _mla_heads(q_ref, kv_parts, o_ref, lane):
    for blk in range(MLA_HEADS // 2):
        outs = []
        for hd in (2 * blk, 2 * blk + 1):
            acc = _attend(q_ref[0, :, _blk(hd)], [k[0, :, _blk(hd)] for k, _ in kv_parts],
                          [v[0, :, _blk(hd)] for _, v in kv_parts], lane)
            outs.append(_normalise(acc))
        o_ref[0, :, _blk(blk)] = _pair(lane, *outs).astype(BF16)


def _diff_lambda(lam_ref, lam_init):
    lp = lam_ref[...]
    return (jnp.exp(jnp.sum(lp[0:1] * lp[1:2], axis=-1, keepdims=True))
            - jnp.exp(jnp.sum(lp[2:3] * lp[3:4], axis=-1, keepdims=True)) + lam_init)


def _diff_heads(q_ref, kv_parts, lam_ref, sub_ref, o_ref, lane, lam_init):
    lam = _diff_lambda(lam_ref, lam_init)
    for blk in range(DIFF_HEADS // 2):
        qb = q_ref[0, :, _blk(blk)]
        ks = [k[0, :, _blk(blk)] for k, _ in kv_parts]
        outs = []
        for u in range(2):
            vs = [v[0, :, _blk(2 * blk + u)] for _, v in kv_parts]
            a1 = _attend(_half_mask(lane, qb, HALF * u, DIFF_QK), ks, vs, lane)
            a2 = _attend(_half_mask(lane, qb, HALF * u + DIFF_QK, DIFF_QK), ks, vs, lane)
            outs.append(_normalise(a1) - lam * _normalise(a2))
        pair = _pair(lane, *outs)
        sq = pair * pair
        lo = jnp.sum(jnp.where(lane < HALF, sq, 0.0), axis=-1, keepdims=True)
        hi = jnp.sum(jnp.where(lane < HALF, 0.0, sq), axis=-1, keepdims=True)
        ms = jnp.where(lane < HALF, lo, hi) * (1.0 / DIFF_V)
        o_ref[0, :, _blk(blk)] = (pair * lax.rsqrt(ms + EPS) * sub_ref[...] * (1.0 - lam_init)).astype(BF16)


def _win_heads(q_ref, k_parts, v_parts, posts, sink_ref, o_ref, lane):
    for blk in range(2):
        qb = q_ref[0, :, _blk(blk)]
        outs = []
        for u in range(WIN_KV_HEADS):
            acc = _attend(_half_mask(lane, qb, HALF * u, HALF), k_parts, [v[:, _blk(u)] for v in v_parts], lane,
                          post=posts, sink=sink_ref[blk + 2 * u] * LOG2E)
            outs.append(_normalise(acc))
        o_ref[0, :, _blk(blk)] = _pair(lane, *outs).astype(BF16)


def _na_heads(q_ref, kv_parts, posts_of_head, o_ref, lane):
    for blk in range(NA_HEADS // 2):
        qb = q_ref[0, :, _blk(blk)]
        outs = []
        for u in range(2):
            hd = 2 * blk + u
            acc = _attend(_half_mask(lane, qb, HALF * u, HALF), [k(_blk(blk)) for k, _ in kv_parts],
                          [v(_blk(hd)) for _, v in kv_parts], lane, post=posts_of_head(hd))
            outs.append(_normalise(acc))
        o_ref[0, :, _blk(blk)] = _pair(lane, *outs).astype(BF16)


def _mla_kernel(q_ref, kl_ref, vl_ref, kc_ref, vc_ref, o_ref):
    _mla_heads(q_ref, [(kl_ref, vl_ref), (kc_ref, vc_ref)], o_ref, _lane_iota(TQ_DENSE))


def _diff_kernel(q_ref, kl_ref, vl_ref, kc_ref, vc_ref, lam_ref, sub_ref, o_ref, *, lam_init):
    _diff_heads(q_ref, [(kl_ref, vl_ref), (kc_ref, vc_ref)], lam_ref, sub_ref, o_ref, _lane_iota(TQ_DENSE),
                lam_init)

def _win_kernel(sink_ref, q_ref, kl_ref, vl_ref, kc_ref, vc_ref, o_ref):
    j = pl.program_id(1)
    start = pl.multiple_of(jnp.clip(j * TQ - WINDOW, 0, SEQ - WIN_N_LOC), WINDOW)
    qpos = j * TQ + lax.broadcasted_iota(jnp.int32, (TQ, WIN_N_LOC), 0)
    kpos = start + lax.broadcasted_iota(jnp.int32, (TQ, WIN_N_LOC), 1)
    allowed = jnp.abs(qpos - kpos) <= WINDOW
    band = lambda s: jnp.where(allowed, s, NEG_INF)
    _win_heads(q_ref, [kl_ref[0, pl.ds(start, WIN_N_LOC), :], kc_ref[0]],
               [vl_ref[0, pl.ds(start, WIN_N_LOC), :], vc_ref[0]], [band, None], sink_ref, o_ref, _lane_iota(TQ))


def _na_kernel(q_ref, kl_ref, vl_ref, kc_ref, vc_ref, bias_ref, o_ref):
    j = pl.program_id(1)
    n_strips = GRID_ROWS // NA_STRIP
    first = jnp.clip(j * NA_STRIP - NA_ROWS // 2, 0, GRID_ROWS - NA_KEY_ROWS)
    start = pl.multiple_of(first * GRID_W, GRID_W)
    variant = (j >= 1).astype(jnp.int32) + (j >= n_strips - 1).astype(jnp.int32)
    parts = [(lambda ls: kl_ref[0, pl.ds(start, NA_N_LOC), ls], lambda ls: vl_ref[0, pl.ds(start, NA_N_LOC), ls]),
             (lambda ls: kc_ref[0, :, ls], lambda ls: vc_ref[0, :, ls])]
    posts = lambda hd: [lambda s: s + bias_ref[variant, hd], None]
    _na_heads(q_ref, parts, posts, o_ref, _lane_iota(TQ))


def _local_kernel(sink_ref, wq, wkl, wvl, wkc, wvc, nq, nkl, nvl, nkc, nvc, bias_ref, oc_ref, od_ref):
    _win_kernel(sink_ref, wq, wkl, wvl, wkc, wvc, oc_ref)
    _na_kernel(nq, nkl, nvl, nkc, nvc, bias_ref, od_ref)


def _local_attention(sink, win, win_ctx, na, na_ctx, bias):
    out_spec = pl.BlockSpec((1, TQ, BRANCH_W), lambda b, j: (b, j, 0))
    return pl.pallas_call(
        _local_kernel,
        grid=(BATCH, SEQ // TQ),
        in_specs=[pl.BlockSpec(memory_space=pltpu.SMEM)] + _qkv_specs(TQ, *win) + _qkv_specs(TQ, *na)
        + [_const_spec(bias.shape)],
        out_specs=[out_spec, out_spec],
        out_shape=[jax.ShapeDtypeStruct((BATCH, SEQ, BRANCH_W), BF16)] * 2,
        compiler_params=_params(2),
        name="local_attention",
    )(sink, *win, *win_ctx, *na, *na_ctx, bias)


def _qkv_specs(tq, q, kl, vl):
    qw, kw, vw = q.shape[-1], kl.shape[-1], vl.shape[-1]
    return [pl.BlockSpec((1, tq, qw), lambda b, j: (b, j, 0)),
            pl.BlockSpec((1, SEQ, kw), lambda b, j: (b, 0, 0)),
            pl.BlockSpec((1, SEQ, vw), lambda b, j: (b, 0, 0)),
            pl.BlockSpec((1, CTX_LEN, kw), lambda b, j: (b, 0, 0)),
            pl.BlockSpec((1, CTX_LEN, vw), lambda b, j: (b, 0, 0))]


def _dense_kernel(mq, mkl, mvl, mkc, mvc, dq, dkl, dvl, dkc, dvc, lam_ref, sub_ref, oa_ref, ob_ref, *, lam_init):
    _mla_kernel(mq, mkl, mvl, mkc, mvc, oa_ref)
    _diff_kernel(dq, dkl, dvl, dkc, dvc, lam_ref, sub_ref, ob_ref, lam_init=lam_init)


def _dense_attention(mla, mla_ctx, diff, diff_ctx, lam, sub, lam_init):
    out_spec = pl.BlockSpec((1, TQ_DENSE, BRANCH_W), lambda b, j: (b, j, 0))
    return pl.pallas_call(
        functools.partial(_dense_kernel, lam_init=lam_init),
        grid=(BATCH, SEQ // TQ_DENSE),
        in_specs=_qkv_specs(TQ_DENSE, *mla) + _qkv_specs(TQ_DENSE, *diff)
        + [_const_spec((4, DIFF_QK)), _const_spec((1, LANES))],
        out_specs=[out_spec, out_spec],
        out_shape=[jax.ShapeDtypeStruct((BATCH, SEQ, BRANCH_W), BF16)] * 2,
        compiler_params=_params(2),
        name="dense_attention",
    )(*mla, *mla_ctx, *diff, *diff_ctx, lam, sub)


def _ctx_kernel(sink_ref, mq, mk, mv, dq, dk, dv, wq, wk, wv, nq, nk, nv, lam_ref, sub_ref,
                oa_ref, ob_ref, oc_ref, od_ref, *, lam_init):
    lane = _lane_iota(CTX_LEN)
    _mla_heads(mq, [(mk, mv)], oa_ref, lane)
    _diff_heads(dq, [(dk, dv)], lam_ref, sub_ref, ob_ref, lane, lam_init)
    _win_heads(wq, [wk[0]], [wv[0]], None, sink_ref, oc_ref, lane)
    _na_heads(nq, [(lambda ls: nk[0, :, ls], lambda ls: nv[0, :, ls])], lambda hd: None, od_ref, lane)


def _ctx_attention(proj_ctx, sink, lam, sub, lam_init):
    tok = lambda width: pl.BlockSpec((1, CTX_LEN, width), lambda b: (b, 0, 0))
    return pl.pallas_call(
        functools.partial(_ctx_kernel, lam_init=lam_init),
        grid=(BATCH,),
        in_specs=[pl.BlockSpec(memory_space=pltpu.SMEM)] + [tok(w) for w in _PROJ_OUT_WIDTHS]
        + [_const_spec((4, DIFF_QK)), _const_spec((1, LANES))],
        out_specs=[tok(BRANCH_W)] * N_BRANCH,
        out_shape=[jax.ShapeDtypeStruct((BATCH, CTX_LEN, BRANCH_W), BF16)] * N_BRANCH,
        compiler_params=_params(1),
        name="ctx_attention",
    )(sink, *proj_ctx, lam, sub)


def _merge_kernel(x_ref, mod_ref, g_ref, oa_ref, ob_ref, oc_ref, od_ref, wg_ref, bg_ref, wb_ref, wo_ref, out_ref):
    x = x_ref[0]
    mod = mod_ref[0]
    h = _adaln(x, g_ref[...], mod[0:1], mod[1:2]).astype(BF16)
    y = None
    for i, o_ref in enumerate((oa_ref, ob_ref, oc_ref, od_ref)):
        cols = slice(D_MODEL * i, D_MODEL * (i + 1))
        gate = jax.nn.sigmoid(_dot(h, wg_ref[:, cols]) + bg_ref[:, cols])
        term = gate * _dot(o_ref[0], wb_ref[BRANCH_W * i:BRANCH_W * (i + 1), :])
        y = term if y is None else y + term
    out_ref[0] = x + mod[2:3] * _dot(y.astype(BF16), wo_ref[...])


def _mlp_kernel(x_ref, mod_ref, g_ref, w1_ref, w2_ref, fn_ref, out_ref, *, final):
    x = x_ref[0]
    mod = mod_ref[0]
    h = _adaln(x, g_ref[...], mod[3:4], mod[4:5]).astype(BF16)
    acc = None
    for i in range(FFN_HIDDEN // D_MODEL):
        cols = slice(D_MODEL * i, D_MODEL * (i + 1))
        u = jnp.square(jnp.maximum(_dot(h, w1_ref[:, cols]), 0.0)).astype(BF16)
        part = _dot(u, w2_ref[cols, :])
        acc = part if acc is None else acc + part
    out = x + mod[5:6] * acc
    out_ref[0] = _rms(out, fn_ref[...]) if final else out


def _token_tile(xs, is_ctx):
    tm = CTX_LEN if is_ctx else TM_LAT
    tok = lambda width: pl.BlockSpec((1, tm, width), lambda j, b: (b, j, 0))
    return xs.shape[1] // tm, tok


def _merge(xs, mods, g, outs, w_gate, b_gate, w_branch, w_out, is_ctx):
    n_tiles, tok = _token_tile(xs, is_ctx)
    return pl.pallas_call(
        _merge_kernel,
        grid=(n_tiles, BATCH),
        in_specs=[tok(D_MODEL), _mod_spec(is_ctx), _const_spec((1, D_MODEL))] + [tok(BRANCH_W)] * N_BRANCH + [
            _const_spec((D_MODEL, N_BRANCH * D_MODEL)),
            _const_spec((1, N_BRANCH * D_MODEL)),
            _const_spec((N_BRANCH * BRANCH_W, D_MODEL)),
            _const_spec((D_MODEL, D_MODEL)),
        ],
        out_specs=tok(D_MODEL),
        out_shape=jax.ShapeDtypeStruct(xs.shape, F32),
        compiler_params=_params(2),
        name="merge_ctx" if is_ctx else "merge",
    )(xs, mods, g, *outs, w_gate, b_gate, w_branch, w_out)


def _mlp(xs, mods, g, w1, w2, final_norm, is_ctx, final):
    n_tiles, tok = _token_tile(xs, is_ctx)
    return pl.pallas_call(
        functools.partial(_mlp_kernel, final=final),
        grid=(n_tiles, BATCH),
        in_specs=[tok(D_MODEL), _mod_spec(is_ctx), _const_spec((1, D_MODEL)),
                  _const_spec((D_MODEL, FFN_HIDDEN)), _const_spec((FFN_HIDDEN, D_MODEL)),
                  _const_spec((1, D_MODEL))],
        out_specs=tok(D_MODEL),
        out_shape=jax.ShapeDtypeStruct(xs.shape, F32),
        compiler_params=_params(2),
        name="mlp_ctx" if is_ctx else ("mlp_final" if final else "mlp"),
    )(xs, mods, g, w1, w2, final_norm)


def _rope_tables():
    t = jnp.arange(SEQ, dtype=jnp.int32)
    row = (t // GRID_W).astype(F32)[:, None]
    col = (t % GRID_W).astype(F32)[:, None]

    def table(dim, lane_start, lane_stop):
        quarter = dim // 4
        inv_freq = ROPE_BASE ** (-jnp.arange(quarter, dtype=F32) / quarter)
        ang = jnp.concatenate([row * inv_freq, row * inv_freq, col * inv_freq, col * inv_freq], axis=-1)
        reps = (lane_stop - lane_start) // dim
        cos = jnp.tile(jnp.cos(ang), (1, reps))
        sin = jnp.tile(jnp.sin(ang), (1, reps))
        lower = (np.arange(lane_stop - lane_start) % (2 * quarter)) < quarter
        pad = lambda a, fill: jnp.pad(a, ((0, 0), (lane_start, LANES - lane_stop)), constant_values=fill)
        return [pad(cos, 1.0), pad(jnp.where(lower, -sin, 0.0), 0.0), pad(jnp.where(lower, 0.0, sin), 0.0)]

    tabs = table(MLA_ROPE, MLA_NOPE, MLA_NOPE + MLA_ROPE) + table(DIFF_QK, 0, LANES) + table(WIN_DIM, 0, LANES)
    return jnp.stack(tabs)


def _na_bias_tables(rpb):
    n_dr = 2 * NA_ROWS - 1
    period = 2 * GRID_W - 1
    pad = GRID_W - NA_COLS
    rp = jnp.pad(rpb.astype(F32), ((0, 0), (0, 0), (pad, pad)))
    flat = jnp.tile(rp, (1, 1, GRID_W + 1))[:, :, :GRID_W * (period + 1)]
    by_col = flat.reshape(NA_HEADS, n_dr, GRID_W, period + 1)[:, :, ::-1, :GRID_W]
    by_query = jnp.pad(jnp.transpose(by_col, (0, 2, 1, 3)), ((0, 0), (0, 0), (NA_STRIP, NA_STRIP), (0, 0)))
    qc = np.arange(GRID_W)[:, None]
    kc = np.arange(GRID_W)[None, :]
    c_start = np.clip(qc - NA_COLS // 2, 0, GRID_W - NA_COLS)
    in_cols = (kc >= c_start) & (kc < c_start + NA_COLS)
    n_strips = GRID_ROWS // NA_STRIP
    tables, masks = [], []
    for strip in (0, 1, n_strips - 1):
        first_key_row = int(np.clip(strip * NA_STRIP - NA_ROWS // 2, 0, GRID_ROWS - NA_KEY_ROWS))
        offset = first_key_row - strip * NA_STRIP + NA_ROWS - 1
        tables.append(jnp.stack(
            [by_query[:, :, offset - dr + NA_STRIP:offset - dr + NA_STRIP + NA_KEY_ROWS] for dr in range(NA_STRIP)],
            axis=1).reshape(NA_HEADS, TQ, NA_N_LOC))
        r = strip * NA_STRIP + np.arange(NA_STRIP)[:, None]
        r_start = np.clip(r - NA_ROWS // 2, 0, GRID_ROWS - NA_ROWS)
        key_row = first_key_row + np.arange(NA_KEY_ROWS)[None, :]
        in_rows = (key_row >= r_start) & (key_row < r_start + NA_ROWS)
        masks.append((in_rows[:, None, :, None] & in_cols[None, :, None, :]).reshape(TQ, NA_N_LOC))
    return jnp.where(np.stack(masks)[:, None], jnp.stack(tables), NEG_INF) * LOG2E


def _pad_heads(w, heads, width, keep):
    w = w.reshape(w.shape[0], heads, width)[:, :, keep[0]:keep[1]]
    return jnp.pad(w, ((0, 0), (0, 0), (0, LANES - (keep[1] - keep[0])))).reshape(w.shape[0], heads * LANES)


_WIN_HEAD_ORDER = (0, 2, 1, 3)


def _prep_w_in(w):
    seg = lambda a, b: w[:, a:b]
    kr = jnp.pad(seg(384, 416), ((0, 0), (MLA_NOPE, LANES - MLA_NOPE - MLA_ROPE)))
    wq = seg(1184, 1440).reshape(D_MODEL, WIN_HEADS, WIN_DIM)[:, np.array(_WIN_HEAD_ORDER)].reshape(D_MODEL, 256)
    return jnp.concatenate([seg(0, 384), kr, seg(416, 1184), wq, seg(1440, 2464)], axis=1).astype(BF16)


def _prep_w_branch(w):
    win = w[512:768].reshape(WIN_HEADS, WIN_DIM, D_MODEL)[np.array(_WIN_HEAD_ORDER)].reshape(256, D_MODEL)
    return jnp.concatenate([w[:512], win, w[768:]], axis=0).astype(BF16)


def _lambda_init(layer):
    return 0.8 - 0.6 * math.exp(-0.3 * layer)


def kernel(x, c, ctx, c_ctx, w_ada, b_ada, norm_mix, norm_ffn, w_in, mla_q_norm, mla_kv_norm, w_uq, w_ukv,
           diff_lambda, diff_subln, win_sink, na_rpb, w_gate, b_gate, w_branch, w_out, w_ff1, w_ff2, final_norm):
    cvec = jnp.concatenate([c, c_ctx[None, :], jnp.zeros((N_MOD_ROWS - BATCH - 1, D_MODEL), F32)], axis=0)
    mods = _modulation(cvec, w_ada, b_ada)
    rope_tab = _rope_tables()
    na_bias = jax.vmap(_na_bias_tables)(na_rpb)
    row = lambda v: v.reshape(1, -1)
    xc = ctx
    for l in range(DEPTH):
        need_ctx = l < DEPTH - 1
        lam_init = _lambda_init(l)
        ukv = w_ukv[l].reshape(MLA_KV_RANK, MLA_HEADS, MLA_NOPE + MLA_V)
        proj_w = (row(norm_mix[l]), _prep_w_in(w_in[l]), row(mla_q_norm[l]), row(mla_kv_norm[l]),
                  _pad_heads(w_uq[l], MLA_HEADS, MLA_NOPE + MLA_ROPE, (0, MLA_NOPE + MLA_ROPE)).astype(BF16),
                  _pad_heads(w_ukv[l], MLA_HEADS, MLA_NOPE + MLA_V, (0, MLA_NOPE)).astype(BF16),
                  ukv[:, :, MLA_NOPE:].reshape(MLA_KV_RANK, MLA_HEADS * MLA_V).astype(BF16))
        mq, mk, mv, dq, dk, dv, wq, wk, wv, nq, nk, nv = _project(x, mods[l], *proj_w, rope_tab, is_ctx=False)
        pc = _project(xc, mods[l], *proj_w, None, is_ctx=True)
        sub = jnp.tile(diff_subln[l], 2).reshape(1, LANES)
        oa, ob = _dense_attention((mq, mk, mv), (pc[1], pc[2]), (dq, dk, dv), (pc[4], pc[5]),
                                  diff_lambda[l], sub, lam_init)
        oc, od = _local_attention(win_sink[l], (wq, wk, wv), (pc[7], pc[8]), (nq, nk, nv), (pc[10], pc[11]),
                                  na_bias[l])
        outs = (oa, ob, oc, od)
        merge_w = (w_gate[l].astype(BF16), row(b_gate[l]), _prep_w_branch(w_branch[l]), w_out[l].astype(BF16))
        mlp_w = (row(norm_ffn[l]), w_ff1[l].astype(BF16), w_ff2[l].astype(BF16), row(final_norm))
        x = _merge(x, mods[l], row(norm_mix[l]), outs, *merge_w, is_ctx=False)
        x = _mlp(x, mods[l], *mlp_w, is_ctx=False, final=not need_ctx)
        if need_ctx:
            outs_c = _ctx_attention(pc, win_sink[l], diff_lambda[l], sub, lam_init)
            xc = _merge(xc, mods[l], row(norm_mix[l]), outs_c, *merge_w, is_ctx=True)
            xc = _mlp(xc, mods[l], *mlp_w, is_ctx=True, final=False)
    return x
```

```python
import functools
import math

import numpy as np
import jax
import jax.numpy as jnp
from jax import lax
from jax.experimental import pallas as pl
from jax.experimental.pallas import tpu as pltpu

D_MODEL = 1024
BATCH = 8
SEQ = 2048
DEPTH = 2
CTX_LEN = 256
GRID_W = 64
GRID_ROWS = SEQ // GRID_W
ROPE_BASE = 10000.0
NEG_INF = -1e30
EPS = 1e-6
N_ADA = 6
LOG2E = math.log2(math.e)

MLA_HEADS = 4
MLA_NOPE = 64
MLA_ROPE = 32
MLA_V = 64
MLA_Q_RANK = 256
MLA_KV_RANK = 128
MLA_SCALE = (MLA_NOPE + MLA_ROPE) ** -0.5
DIFF_HEADS = 4
DIFF_QK = 32
DIFF_V = 64
DIFF_SCALE = DIFF_QK ** -0.5
WIN_HEADS = 4
WIN_KV_HEADS = 2
WIN_DIM = 64
WINDOW = 128
WIN_SCALE = WIN_DIM ** -0.5
NA_HEADS = 4
NA_DIM = 64
NA_ROWS = 8
NA_COLS = 16
NA_SCALE = NA_DIM ** -0.5
FFN_HIDDEN = 4 * D_MODEL
N_BRANCH = 4
BRANCH_W = 256

LANES = 128
HALF = LANES // 2
VMEM_LIMIT = 56 * 1024 * 1024

BF16 = jnp.bfloat16
F32 = jnp.float32

P_CQ, P_CKV, P_KR = 0, 256, 384
P_DQ, P_DK, P_DV = 512, 768, 1024
P_WQ, P_WK, P_WV = 1280, 1536, 1664
P_NQ, P_NK, P_NV = 1792, 2048, 2304
P_WIDTH = 2560

TM_PROJ = 512
TQ = 256
TQ_DENSE = 512
TM_LAT = 512
N_MOD_ROWS = 16
CTX_MOD_ROW = BATCH

NA_STRIP = TQ // GRID_W
NA_KEY_ROWS = 12
NA_N_LOC = NA_KEY_ROWS * GRID_W
NA_BIAS_VARIANTS = 3
WIN_N_LOC = TQ + 2 * WINDOW


def _const_spec(shape):
    nd = len(shape)
    return pl.BlockSpec(shape, lambda *_: (0,) * nd, pipeline_mode=pl.Buffered(1))


def _params(n_axes):
    return pltpu.CompilerParams(dimension_semantics=("arbitrary",) * n_axes, vmem_limit_bytes=VMEM_LIMIT)


def _rms(x, g):
    return x * lax.rsqrt(jnp.mean(x * x, axis=-1, keepdims=True) + EPS) * g


def _adaln(x, g, shift, scale):
    return _rms(x, g) * (1.0 + scale) + shift


def _dot(a, b):
    return jnp.dot(a, b, preferred_element_type=F32)


def _dot_nt(a, b):
    return lax.dot_general(a, b, (((1,), (1,)), ((), ())), preferred_element_type=F32)


def _lane_iota(rows):
    return lax.broadcasted_iota(jnp.int32, (rows, LANES), 1)


def _blk(i):
    return slice(LANES * i, LANES * (i + 1))


def _mod_kernel(c_ref, w_ref, b_ref, o_ref):
    c = c_ref[...]
    act = c * jax.nn.sigmoid(c)
    o_ref[0] = jnp.dot(act, w_ref[0], precision=lax.Precision.HIGHEST, preferred_element_type=F32) + b_ref[0]


def _modulation(cvec, w_ada, b_ada):
    tn = 1536
    n_ada = N_ADA * D_MODEL
    out = pl.pallas_call(
        _mod_kernel,
        grid=(DEPTH, n_ada // tn),
        in_specs=[
            pl.BlockSpec((N_MOD_ROWS, D_MODEL), lambda l, n: (0, 0)),
            pl.BlockSpec((1, D_MODEL, tn), lambda l, n: (l, 0, n)),
            pl.BlockSpec((1, 1, tn), lambda l, n: (l, 0, n)),
        ],
        out_specs=pl.BlockSpec((1, N_MOD_ROWS, tn), lambda l, n: (l, 0, n)),
        out_shape=jax.ShapeDtypeStruct((DEPTH, N_MOD_ROWS, n_ada), F32),
        compiler_params=_params(2),
        name="modulation",
    )(cvec, w_ada, b_ada.reshape(DEPTH, 1, n_ada))
    return out.reshape(DEPTH, N_MOD_ROWS, N_ADA, D_MODEL)


def _mod_spec(is_ctx):
    if is_ctx:
        return pl.BlockSpec((1, N_ADA, D_MODEL), lambda j, b: (CTX_MOD_ROW, 0, 0))
    return pl.BlockSpec((1, N_ADA, D_MODEL), lambda j, b: (b, 0, 0))


def _proj_kernel(x_ref, mod_ref, g_ref, win_ref, qn_ref, kvn_ref, wuq_ref, wuk_ref, wuv_ref, *rest, rotary):
    if rotary:
        rope_ref, rest = rest[0], rest[1:]
    mq_ref, mk_ref, mv_ref, dq_ref, dk_ref, dv_ref, wq_ref, wk_ref, wv_ref, nq_ref, nk_ref, nv_ref = rest
    mod = mod_ref[0]
    h = _adaln(x_ref[0], g_ref[...], mod[0:1], mod[1:2]).astype(BF16)
    lane = _lane_iota(h.shape[0])

    def seg(start, width):
        return _dot(h, win_ref[:, start:start + width])

    def rope(x, base, quarter):
        if not rotary:
            return x
        return (x * rope_ref[base]
                + pltpu.roll(x, LANES - quarter, 1) * rope_ref[base + 1]
                + pltpu.roll(x, quarter, 1) * rope_ref[base + 2])

    def put(ref, val, rope_base=None, quarter=None, scale=None):
        for i in range(val.shape[1] // LANES):
            blk = val[:, _blk(i)]
            if rope_base is not None:
                blk = rope(blk, rope_base, quarter)
            if scale is not None:
                blk = blk * scale
            ref[0, :, _blk(i)] = blk.astype(BF16)

    def put_values(ref, val):
        for hd in range(val.shape[1] // HALF):
            blk = val[:, _blk(hd // 2)]
            if hd % 2:
                blk = pltpu.roll(blk, HALF, 1)
            ref[0, :, _blk(hd)] = jnp.where(lane < HALF, blk, 1.0).astype(BF16)

    cq = _rms(seg(P_CQ, MLA_Q_RANK), qn_ref[...]).astype(BF16)
    put(mq_ref, _dot(cq, wuq_ref[...]), 0, MLA_ROPE // 4, MLA_SCALE * LOG2E)
    ckv = _rms(seg(P_CKV, MLA_KV_RANK), kvn_ref[...]).astype(BF16)
    kr = rope(seg(P_KR, LANES), 0, MLA_ROPE // 4)
    kn = _dot(ckv, wuk_ref[...])
    for i in range(MLA_HEADS):
        mk_ref[0, :, _blk(i)] = (kn[:, _blk(i)] + kr).astype(BF16)
    put_values(mv_ref, _dot(ckv, wuv_ref[...]))
    put(dq_ref, seg(P_DQ, 256), 3, DIFF_QK // 4, DIFF_SCALE * LOG2E)
    put(dk_ref, seg(P_DK, 256), 3, DIFF_QK // 4)
    put_values(dv_ref, seg(P_DV, 256))
    put(wq_ref, seg(P_WQ, 256), 6, WIN_DIM // 4, WIN_SCALE * LOG2E)
    put(wk_ref, seg(P_WK, 128), 6, WIN_DIM // 4)
    put_values(wv_ref, seg(P_WV, 128))
    put(nq_ref, seg(P_NQ, 256), scale=NA_SCALE * LOG2E)
    put(nk_ref, seg(P_NK, 256))
    put_values(nv_ref, seg(P_NV, 256))


_PROJ_OUT_WIDTHS = (512, 512, 512, 256, 256, 512, 256, 128, 256, 256, 256, 512)


def _project(xs, mods, g, w_in_p, q_norm, kv_norm, w_uq_p, w_uk_p, w_uv, rope_tab, is_ctx):
    n_tok = xs.shape[1]
    tm = min(TM_PROJ, n_tok)
    tok = lambda width: pl.BlockSpec((1, tm, width), lambda j, b: (b, j, 0))
    in_specs = [
        tok(D_MODEL),
        _mod_spec(is_ctx),
        _const_spec((1, D_MODEL)),
        _const_spec((D_MODEL, P_WIDTH)),
        _const_spec((1, MLA_Q_RANK)),
        _const_spec((1, MLA_KV_RANK)),
        _const_spec((MLA_Q_RANK, MLA_HEADS * LANES)),
        _const_spec((MLA_KV_RANK, MLA_HEADS * LANES)),
        _const_spec((MLA_KV_RANK, MLA_HEADS * MLA_V)),
    ]
    args = [xs, mods, g, w_in_p, q_norm, kv_norm, w_uq_p, w_uk_p, w_uv]
    if not is_ctx:
        in_specs.append(pl.BlockSpec((9, tm, LANES), lambda j, b: (0, j, 0)))
        args.append(rope_tab)
    return pl.pallas_call(
        functools.partial(_proj_kernel, rotary=not is_ctx),
        grid=(n_tok // tm, BATCH),
        in_specs=in_specs,
        out_specs=[tok(w) for w in _PROJ_OUT_WIDTHS],
        out_shape=[jax.ShapeDtypeStruct((BATCH, n_tok, w), BF16) for w in _PROJ_OUT_WIDTHS],
        compiler_params=_params(2),
        name="project_ctx" if is_ctx else "project",
    )(*args)


def _attend(q, ks, vs, lane, post=None, sink=None):
    ss = []
    for i, k in enumerate(ks):
        s = _dot_nt(q, k)
        if post is not None and post[i] is not None:
            s = post[i](s)
        ss.append(s)
    m = ss[0].max(axis=-1, keepdims=True)
    for s in ss[1:]:
        m = jnp.maximum(m, s.max(axis=-1, keepdims=True))
    if sink is not None:
        m = jnp.maximum(m, sink)
    acc = None
    for s, v in zip(ss, vs):
        part = _dot(jnp.exp2(s - m).astype(BF16), v)
        acc = part if acc is None else acc + part
    if sink is not None:
        acc = acc + jnp.where(lane >= HALF, jnp.exp2(sink - m), 0.0)
    return acc


def _normalise(acc):
    return acc * (1.0 / pltpu.roll(acc, HALF, 1))


def _pair(lane, even, odd):
    return jnp.where(lane < HALF, even, pltpu.roll(odd, HALF, 1))


def _half_mask(lane, q, start, width):
    return jnp.where((lane >= start) & (lane < start + width), q, jnp.zeros_like(q))


def _mla_heads(q_ref, kv_parts, o_ref, lane):
    for blk in range(MLA_HEADS // 2):
        outs = []
        for hd in (2 * blk, 2 * blk + 1):
            acc = _attend(q_ref[0, :, _blk(hd)], [k[0, :, _blk(hd)] for k, _ in kv_parts],
                          [v[0, :, _blk(hd)] for _, v in kv_parts], lane)
            outs.append(_normalise(acc))
        o_ref[0, :, _blk(blk)] = _pair(lane, *outs).astype(BF16)


def _diff_lambda(lam_ref, lam_init):
    lp = lam_ref[...]
    return (jnp.exp(jnp.sum(lp[0:1] * lp[1:2], axis=-1, keepdims=True))
            - jnp.exp(jnp.sum(lp[2:3] * lp[3:4], axis=-1, keepdims=True)) + lam_init)


def _diff_heads(q_ref, kv_parts, lam_ref, sub_ref, o_ref, lane, lam_init):
    lam = _diff_lambda(lam_ref, lam_init)
    for blk in range(DIFF_HEADS // 2):
        qb = q_ref[0, :, _blk(blk)]
        ks = [k[0, :, _blk(blk)] for k, _ in kv_parts]
        outs = []
        for u in range(2):
            vs = [v[0, :, _blk(2 * blk + u)] for _, v in kv_parts]
            a1 = _attend(_half_mask(lane, qb, HALF * u, DIFF_QK), ks, vs, lane)
            a2 = _attend(_half_mask(lane, qb, HALF * u + DIFF_QK, DIFF_QK), ks, vs, lane)
            outs.append(_normalise(a1) - lam * _normalise(a2))
        pair = _pair(lane, *outs)
        sq = pair * pair
        lo = jnp.sum(jnp.where(lane < HALF, sq, 0.0), axis=-1, keepdims=True)
        hi = jnp.sum(jnp.where(lane < HALF, 0.0, sq), axis=-1, keepdims=True)
        ms = jnp.where(lane < HALF, lo, hi) * (1.0 / DIFF_V)
        o_ref[0, :, _blk(blk)] = (pair * lax.rsqrt(ms + EPS) * sub_ref[...] * (1.0 - lam_init)).astype(BF16)


def _win_heads(q_ref, k_parts, v_parts, posts, sink_ref, o_ref, lane):
    for blk in range(2):
        qb = q_ref[0, :, _blk(blk)]
        outs = []
        for u in range(WIN_KV_HEADS):
            acc = _attend(_half_mask(lane, qb, HALF * u, HALF), k_parts, [v[:, _blk(u)] for v in v_parts], lane,
                          post=posts, sink=sink_ref[blk + 2 * u] * LOG2E)
            outs.append(_normalise(acc))
        o_ref[0, :, _blk(blk)] = _pair(lane, *outs).astype(BF16)


def _na_heads(q_ref, kv_parts, posts_of_head, o_ref, lane):
    for blk in range(NA_HEADS // 2):
        qb = q_ref[0, :, _blk(blk)]
        outs = []
        for u in range(2):
            hd = 2 * blk + u
            acc = _attend(_half_mask(lane, qb, HALF * u, HALF), [k(_blk(blk)) for k, _ in kv_parts],
                          [v(_blk(hd)) for _, v in kv_parts], lane, post=posts_of_head(hd))
            outs.append(_normalise(acc))
        o_ref[0, :, _blk(blk)] = _pair(lane, *outs).astype(BF16)


def _mla_kernel(q_ref, kl_ref, vl_ref, kc_ref, vc_ref, o_ref):
    _mla_heads(q_ref, [(kl_ref, vl_ref), (kc_ref, vc_ref)], o_ref, _lane_iota(TQ_DENSE))


def _diff_kernel(q_ref, kl_ref, vl_ref, kc_ref, vc_ref, lam_ref, sub_ref, o_ref, *, lam_init):
    _diff_heads(q_ref, [(kl_ref, vl_ref), (kc_ref, vc_ref)], lam_ref, sub_ref, o_ref, _lane_iota(TQ_DENSE),
                lam_init)

def _win_kernel(sink_ref, q_ref, kl_ref, vl_ref, kc_ref, vc_ref, o_ref):
    j = pl.program_id(1)
    start = pl.multiple_of(jnp.clip(j * TQ - WINDOW, 0, SEQ - WIN_N_LOC), WINDOW)
    qpos = j * TQ + lax.broadcasted_iota(jnp.int32, (TQ, WIN_N_LOC), 0)
    kpos = start + lax.broadcasted_iota(jnp.int32, (TQ, WIN_N_LOC), 1)
    allowed = jnp.abs(qpos - kpos) <= WINDOW
    band = lambda s: jnp.where(allowed, s, NEG_INF)
    _win_heads(q_ref, [kl_ref[0, pl.ds(start, WIN_N_LOC), :], kc_ref[0]],
               [vl_ref[0, pl.ds(start, WIN_N_LOC), :], vc_ref[0]], [band, None], sink_ref, o_ref, _lane_iota(TQ))


def _na_kernel(q_ref, kl_ref, vl_ref, kc_ref, vc_ref, bias_ref, o_ref):
    j = pl.program_id(1)
    n_strips = GRID_ROWS // NA_STRIP
    first = jnp.clip(j * NA_STRIP - NA_ROWS // 2, 0, GRID_ROWS - NA_KEY_ROWS)
    start = pl.multiple_of(first * GRID_W, GRID_W)
    variant = (j >= 1).astype(jnp.int32) + (j >= n_strips - 1).astype(jnp.int32)
    parts = [(lambda ls: kl_ref[0, pl.ds(start, NA_N_LOC), ls], lambda ls: vl_ref[0, pl.ds(start, NA_N_LOC), ls]),
             (lambda ls: kc_ref[0, :, ls], lambda ls: vc_ref[0, :, ls])]
    posts = lambda hd: [lambda s: s + bias_ref[variant, hd], None]
    _na_heads(q_ref, parts, posts, o_ref, _lane_iota(TQ))


def _local_kernel(sink_ref, wq, wkl, wvl, wkc, wvc, nq, nkl, nvl, nkc, nvc, bias_ref, oc_ref, od_ref):
    _win_kernel(sink_ref, wq, wkl, wvl, wkc, wvc, oc_ref)
    _na_kernel(nq, nkl, nvl, nkc, nvc, bias_ref, od_ref)


def _local_attention(sink, win, win_ctx, na, na_ctx, bias):
    out_spec = pl.BlockSpec((1, TQ, BRANCH_W), lambda b, j: (b, j, 0))
    return pl.pallas_call(
        _local_kernel,
        grid=(BATCH, SEQ // TQ),
        in_specs=[pl.BlockSpec(memory_space=pltpu.SMEM)] + _qkv_specs(TQ, *win) + _qkv_specs(TQ, *na)
        + [_const_spec(bias.shape)],
        out_specs=[out_spec, out_spec],
        out_shape=[jax.ShapeDtypeStruct((BATCH, SEQ, BRANCH_W), BF16)] * 2,
        compiler_params=_params(2),
        name="local_attention",
    )(sink, *win, *win_ctx, *na, *na_ctx, bias)


def _qkv_specs(tq, q, kl, vl):
    qw, kw, vw = q.shape[-1], kl.shape[-1], vl.shape[-1]
    return [pl.BlockSpec((1, tq, qw), lambda b, j: (b, j, 0)),
            pl.BlockSpec((1, SEQ, kw), lambda b, j: (b, 0, 0)),
            pl.BlockSpec((1, SEQ, vw), lambda b, j: (b, 0, 0)),
            pl.BlockSpec((1, CTX_LEN, kw), lambda b, j: (b, 0, 0)),
            pl.BlockSpec((1, CTX_LEN, vw), lambda b, j: (b, 0, 0))]


def _dense_kernel(mq, mkl, mvl, mkc, mvc, dq, dkl, dvl, dkc, dvc, lam_ref, sub_ref, oa_ref, ob_ref, *, lam_init):
    _mla_kernel(mq, mkl, mvl, mkc, mvc, oa_ref)
    _diff_kernel(dq, dkl, dvl, dkc, dvc, lam_ref, sub_ref, ob_ref, lam_init=lam_init)


def _dense_attention(mla, mla_ctx, diff, diff_ctx, lam, sub, lam_init):
    out_spec = pl.BlockSpec((1, TQ_DENSE, BRANCH_W), lambda b, j: (b, j, 0))
    return pl.pallas_call(
        functools.partial(_dense_kernel, lam_init=lam_init),
        grid=(BATCH, SEQ // TQ_DENSE),
        in_specs=_qkv_specs(TQ_DENSE, *mla) + _qkv_specs(TQ_DENSE, *diff)
        + [_const_spec((4, DIFF_QK)), _const_spec((1, LANES))],
        out_specs=[out_spec, out_spec],
        out_shape=[jax.ShapeDtypeStruct((BATCH, SEQ, BRANCH_W), BF16)] * 2,
        compiler_params=_params(2),
        name="dense_attention",
    )(*mla, *mla_ctx, *diff, *diff_ctx, lam, sub)


def _ctx_kernel(sink_ref, mq, mk, mv, dq, dk, dv, wq, wk, wv, nq, nk, nv, lam_ref, sub_ref,
                oa_ref, ob_ref, oc_ref, od_ref, *, lam_init):
    lane = _lane_iota(CTX_LEN)
    _mla_heads(mq, [(mk, mv)], oa_ref, lane)
    _diff_heads(dq, [(dk, dv)], lam_ref, sub_ref, ob_ref, lane, lam_init)
    _win_heads(wq, [wk[0]], [wv[0]], None, sink_ref, oc_ref, lane)
    _na_heads(nq, [(lambda ls: nk[0, :, ls], lambda ls: nv[0, :, ls])], lambda hd: None, od_ref, lane)


def _ctx_attention(proj_ctx, sink, lam, sub, lam_init):
    tok = lambda width: pl.BlockSpec((1, CTX_LEN, width), lambda b: (b, 0, 0))
    return pl.pallas_call(
        functools.partial(_ctx_kernel, lam_init=lam_init),
        grid=(BATCH,),
        in_specs=[pl.BlockSpec(memory_space=pltpu.SMEM)] + [tok(w) for w in _PROJ_OUT_WIDTHS]
        + [_const_spec((4, DIFF_QK)), _const_spec((1, LANES))],
        out_specs=[tok(BRANCH_W)] * N_BRANCH,
        out_shape=[jax.ShapeDtypeStruct((BATCH, CTX_LEN, BRANCH_W), BF16)] * N_BRANCH,
        compiler_params=_params(1),
        name="ctx_attention",
    )(sink, *proj_ctx, lam, sub)


def _channel_kernel(x_ref, mod_ref, gm_ref, oa_ref, ob_ref, oc_ref, od_ref, wg_ref, bg_ref, wb_ref, wo_ref,
                    gf_ref, w1_ref, w2_ref, fn_ref, out_ref, *, final):
    x = x_ref[0]
    mod = mod_ref[0]
    h = _adaln(x, gm_ref[...], mod[0:1], mod[1:2]).astype(BF16)
    y = None
    for i, o_ref in enumerate((oa_ref, ob_ref, oc_ref, od_ref)):
        cols = slice(D_MODEL * i, D_MODEL * (i + 1))
        gate = jax.nn.sigmoid(_dot(h, wg_ref[:, cols]) + bg_ref[:, cols])
        term = gate * _dot(o_ref[0], wb_ref[BRANCH_W * i:BRANCH_W * (i + 1), :])
        y = term if y is None else y + term
    x = x + mod[2:3] * _dot(y.astype(BF16), wo_ref[...])
    h = _adaln(x, gf_ref[...], mod[3:4], mod[4:5]).astype(BF16)
    acc = None
    for i in range(FFN_HIDDEN // D_MODEL):
        cols = slice(D_MODEL * i, D_MODEL * (i + 1))
        u = jnp.square(jnp.maximum(_dot(h, w1_ref[:, cols]), 0.0)).astype(BF16)
        part = _dot(u, w2_ref[cols, :])
        acc = part if acc is None else acc + part
    out = x + mod[5:6] * acc
    out_ref[0] = _rms(out, fn_ref[...]) if final else out


def _channel_mix(xs, mods, outs, g_mix, w_gate, b_gate, w_branch, w_out, g_ffn, w1, w2, final_norm, is_ctx, final):
    tm = CTX_LEN if is_ctx else TM_LAT
    tok = lambda width: pl.BlockSpec((1, tm, width), lambda j, b: (b, j, 0))
    return pl.pallas_call(
        functools.partial(_channel_kernel, final=final),
        grid=(xs.shape[1] // tm, BATCH),
        in_specs=[tok(D_MODEL), _mod_spec(is_ctx), _const_spec((1, D_MODEL))] + [tok(BRANCH_W)] * N_BRANCH + [
            _const_spec((D_MODEL, N_BRANCH * D_MODEL)),
            _const_spec((1, N_BRANCH * D_MODEL)),
            _const_spec((N_BRANCH * BRANCH_W, D_MODEL)),
            _const_spec((D_MODEL, D_MODEL)),
            _const_spec((1, D_MODEL)),
            _const_spec((D_MODEL, FFN_HIDDEN)),
            _const_spec((FFN_HIDDEN, D_MODEL)),
            _const_spec((1, D_MODEL)),
        ],
        out_specs=tok(D_MODEL),
        out_shape=jax.ShapeDtypeStruct(xs.shape, F32),
        compiler_params=_params(2),
        name="channel_mix_ctx" if is_ctx else ("channel_mix_final" if final else "channel_mix"),
    )(xs, mods, g_mix, *outs, w_gate, b_gate, w_branch, w_out, g_ffn, w1, w2, final_norm)


def _rope_tables():
    t = jnp.arange(SEQ, dtype=jnp.int32)
    row = (t // GRID_W).astype(F32)[:, None]
    col = (t % GRID_W).astype(F32)[:, None]

    def table(dim, lane_start, lane_stop):
        quarter = dim // 4
        inv_freq = ROPE_BASE ** (-jnp.arange(quarter, dtype=F32) / quarter)
        ang = jnp.concatenate([row * inv_freq, row * inv_freq, col * inv_freq, col * inv_freq], axis=-1)
        reps = (lane_stop - lane_start) // dim
        cos = jnp.tile(jnp.cos(ang), (1, reps))
        sin = jnp.tile(jnp.sin(ang), (1, reps))
        lower = (np.arange(lane_stop - lane_start) % (2 * quarter)) < quarter
        pad = lambda a, fill: jnp.pad(a, ((0, 0), (lane_start, LANES - lane_stop)), constant_values=fill)
        return [pad(cos, 1.0), pad(jnp.where(lower, -sin, 0.0), 0.0), pad(jnp.where(lower, 0.0, sin), 0.0)]

    tabs = table(MLA_ROPE, MLA_NOPE, MLA_NOPE + MLA_ROPE) + table(DIFF_QK, 0, LANES) + table(WIN_DIM, 0, LANES)
    return jnp.stack(tabs)


def _na_bias_tables(rpb):
    n_dr = 2 * NA_ROWS - 1
    period = 2 * GRID_W - 1
    pad = GRID_W - NA_COLS
    rp = jnp.pad(rpb.astype(F32), ((0, 0), (0, 0), (pad, pad)))
    flat = jnp.tile(rp, (1, 1, GRID_W + 1))[:, :, :GRID_W * (period + 1)]
    by_col = flat.reshape(NA_HEADS, n_dr, GRID_W, period + 1)[:, :, ::-1, :GRID_W]
    by_query = jnp.pad(jnp.transpose(by_col, (0, 2, 1, 3)), ((0, 0), (0, 0), (NA_STRIP, NA_STRIP), (0, 0)))
    by_query = by_query.reshape(NA_HEADS, GRID_W, (n_dr + 2 * NA_STRIP) * GRID_W)
    qc = np.arange(GRID_W)[:, None]
    kc = np.arange(GRID_W)[None, :]
    c_start = np.clip(qc - NA_COLS // 2, 0, GRID_W - NA_COLS)
    in_cols = (kc >= c_start) & (kc < c_start + NA_COLS)
    n_strips = GRID_ROWS // NA_STRIP
    tables, masks = [], []
    for strip in (0, 1, n_strips - 1):
        first_key_row = int(np.clip(strip * NA_STRIP - NA_ROWS // 2, 0, GRID_ROWS - NA_KEY_ROWS))
        offset = first_key_row - strip * NA_STRIP + NA_ROWS - 1
        first_lane = lambda dr: (offset - dr + NA_STRIP) * GRID_W
        tables.append(jnp.stack(
            [by_query[:, :, first_lane(dr):first_lane(dr) + NA_N_LOC] for dr in range(NA_STRIP)],
            axis=1).reshape(NA_HEADS, TQ, NA_N_LOC))
        r = strip * NA_STRIP + np.arange(NA_STRIP)[:, None]
        r_start = np.clip(r - NA_ROWS // 2, 0, GRID_ROWS - NA_ROWS)
        key_row = first_key_row + np.arange(NA_KEY_ROWS)[None, :]
        in_rows = (key_row >= r_start) & (key_row < r_start + NA_ROWS)
        masks.append((in_rows[:, None, :, None] & in_cols[None, :, None, :]).reshape(TQ, NA_N_LOC))
    return jnp.where(np.stack(masks)[:, None], jnp.stack(tables), NEG_INF) * LOG2E


def _pad_heads(w, heads, width, keep):
    w = w.reshape(w.shape[0], heads, width)[:, :, keep[0]:keep[1]]
    return jnp.pad(w, ((0, 0), (0, 0), (0, LANES - (keep[1] - keep[0])))).reshape(w.shape[0], heads * LANES)


_WIN_HEAD_ORDER = (0, 2, 1, 3)


def _prep_w_in(w):
    seg = lambda a, b: w[:, a:b]
    kr = jnp.pad(seg(384, 416), ((0, 0), (MLA_NOPE, LANES - MLA_NOPE - MLA_ROPE)))
    wq = seg(1184, 1440).reshape(D_MODEL, WIN_HEADS, WIN_DIM)[:, np.array(_WIN_HEAD_ORDER)].reshape(D_MODEL, 256)
    return jnp.concatenate([seg(0, 384), kr, seg(416, 1184), wq, seg(1440, 2464)], axis=1).astype(BF16)


def _prep_w_branch(w):
    win = w[512:768].reshape(WIN_HEADS, WIN_DIM, D_MODEL)[np.array(_WIN_HEAD_ORDER)].reshape(256, D_MODEL)
    return jnp.concatenate([w[:512], win, w[768:]], axis=0).astype(BF16)


def _lambda_init(layer):
    return 0.8 - 0.6 * math.exp(-0.3 * layer)


def kernel(x, c, ctx, c_ctx, w_ada, b_ada, norm_mix, norm_ffn, w_in, mla_q_norm, mla_kv_norm, w_uq, w_ukv,
           diff_lambda, diff_subln, win_sink, na_rpb, w_gate, b_gate, w_branch, w_out, w_ff1, w_ff2, final_norm):
    cvec = jnp.concatenate([c, c_ctx[None, :], jnp.zeros((N_MOD_ROWS - BATCH - 1, D_MODEL), F32)], axis=0)
    mods = _modulation(cvec, w_ada, b_ada)
    rope_tab = _rope_tables()
    na_bias = jax.vmap(_na_bias_tables)(na_rpb)
    row = lambda v: v.reshape(1, -1)
    xc = ctx
    for l in range(DEPTH):
        need_ctx = l < DEPTH - 1
        lam_init = _lambda_init(l)
        ukv = w_ukv[l].reshape(MLA_KV_RANK, MLA_HEADS, MLA_NOPE + MLA_V)
        proj_w = (row(norm_mix[l]), _prep_w_in(w_in[l]), row(mla_q_norm[l]), row(mla_kv_norm[l]),
                  _pad_heads(w_uq[l], MLA_HEADS, MLA_NOPE + MLA_ROPE, (0, MLA_NOPE + MLA_ROPE)).astype(BF16),
                  _pad_heads(w_ukv[l], MLA_HEADS, MLA_NOPE + MLA_V, (0, MLA_NOPE)).astype(BF16),
                  ukv[:, :, MLA_NOPE:].reshape(MLA_KV_RANK, MLA_HEADS * MLA_V).astype(BF16))
        mq, mk, mv, dq, dk, dv, wq, wk, wv, nq, nk, nv = _project(x, mods[l], *proj_w, rope_tab, is_ctx=False)
        pc = _project(xc, mods[l], *proj_w, None, is_ctx=True)
        sub = jnp.tile(diff_subln[l], 2).reshape(1, LANES)
        oa, ob = _dense_attention((mq, mk, mv), (pc[1], pc[2]), (dq, dk, dv), (pc[4], pc[5]),
                                  diff_lambda[l], sub, lam_init)
        oc, od = _local_attention(win_sink[l], (wq, wk, wv), (pc[7], pc[8]), (nq, nk, nv), (pc[10], pc[11]),
                                  na_bias[l])
        outs = (oa, ob, oc, od)
        mix_w = (row(norm_mix[l]), w_gate[l].astype(BF16), row(b_gate[l]), _prep_w_branch(w_branch[l]),
                 w_out[l].astype(BF16), row(norm_ffn[l]), w_ff1[l].astype(BF16), w_ff2[l].astype(BF16),
                 row(final_norm))
        x = _channel_mix(x, mods[l], outs, *mix_w, is_ctx=False, final=not need_ctx)
        if need_ctx:
            outs_c = _ctx_attention(pc, win_sink[l], diff_lambda[l], sub, lam_init)
            xc = _channel_mix(xc, mods[l], outs_c, *mix_w, is_ctx=True, final=False)
    return x
```

```python
import functools
import math

import numpy as np
import jax
import jax.numpy as jnp
from jax import lax
from jax.experimental import pallas as pl
from jax.experimental.pallas import tpu as pltpu

D_MODEL = 1024
BATCH = 8
SEQ = 2048
DEPTH = 2
CTX_LEN = 256
GRID_W = 64
GRID_ROWS = SEQ // GRID_W
ROPE_BASE = 10000.0
NEG_INF = -1e30
EPS = 1e-6
N_ADA = 6
LOG2E = math.log2(math.e)

MLA_HEADS = 4
MLA_NOPE = 64
MLA_ROPE = 32
MLA_V = 64
MLA_Q_RANK = 256
MLA_KV_RANK = 128
MLA_SCALE = (MLA_NOPE + MLA_ROPE) ** -0.5
DIFF_HEADS = 4
DIFF_QK = 32
DIFF_V = 64
DIFF_SCALE = DIFF_QK ** -0.5
WIN_HEADS = 4
WIN_KV_HEADS = 2
WIN_DIM = 64
WINDOW = 128
WIN_SCALE = WIN_DIM ** -0.5
NA_HEADS = 4
NA_DIM = 64
NA_ROWS = 8
NA_COLS = 16
NA_SCALE = NA_DIM ** -0.5
FFN_HIDDEN = 4 * D_MODEL
N_BRANCH = 4
BRANCH_W = 256

LANES = 128
HALF = LANES // 2
VMEM_LIMIT = 56 * 1024 * 1024

BF16 = jnp.bfloat16
F32 = jnp.float32

P_CQ, P_CKV, P_KR = 0, 256, 384
P_DQ, P_DK, P_DV = 512, 768, 1024
P_WQ, P_WK, P_WV = 1280, 1536, 1664
P_NQ, P_NK, P_NV = 1792, 2048, 2304
P_WIDTH = 2560

TM_PROJ = 512
TQ = 256
TQ_DENSE = 512
TM_LAT = 512
N_MOD_ROWS = 16
CTX_MOD_ROW = BATCH

NA_STRIP = TQ // GRID_W
NA_KEY_ROWS = 12
NA_N_LOC = NA_KEY_ROWS * GRID_W
NA_BIAS_VARIANTS = 3
WIN_N_LOC = TQ + 2 * WINDOW


def _const_spec(shape):
    nd = len(shape)
    return pl.BlockSpec(shape, lambda *_: (0,) * nd, pipeline_mode=pl.Buffered(1))


def _params(n_axes):
    return pltpu.CompilerParams(dimension_semantics=("arbitrary",) * n_axes, vmem_limit_bytes=VMEM_LIMIT)


def _rms(x, g):
    return x * lax.rsqrt(jnp.mean(x * x, axis=-1, keepdims=True) + EPS) * g


def _adaln(x, g, shift, scale):
    return _rms(x, g) * (1.0 + scale) + shift


def _dot(a, b):
    return jnp.dot(a, b, preferred_element_type=F32)


def _dot_nt(a, b):
    return lax.dot_general(a, b, (((1,), (1,)), ((), ())), preferred_element_type=F32)


def _lane_iota(rows):
    return lax.broadcasted_iota(jnp.int32, (rows, LANES), 1)


def _blk(i):
    return slice(LANES * i, LANES * (i + 1))


def _mod_kernel(c_ref, w_ref, b_ref, o_ref):
    c = c_ref[...]
    act = (c * jax.nn.sigmoid(c)).astype(BF16)
    o_ref[0] = _dot(act, w_ref[0].astype(BF16)) + b_ref[0]


def _modulation(cvec, w_ada, b_ada):
    tn = 1536
    n_ada = N_ADA * D_MODEL
    out = pl.pallas_call(
        _mod_kernel,
        grid=(DEPTH, n_ada // tn),
        in_specs=[
            pl.BlockSpec((N_MOD_ROWS, D_MODEL), lambda l, n: (0, 0)),
            pl.BlockSpec((1, D_MODEL, tn), lambda l, n: (l, 0, n)),
            pl.BlockSpec((1, 1, tn), lambda l, n: (l, 0, n)),
        ],
        out_specs=pl.BlockSpec((1, N_MOD_ROWS, tn), lambda l, n: (l, 0, n)),
        out_shape=jax.ShapeDtypeStruct((DEPTH, N_MOD_ROWS, n_ada), F32),
        compiler_params=_params(2),
        name="modulation",
    )(cvec, w_ada, b_ada.reshape(DEPTH, 1, n_ada))
    return out.reshape(DEPTH, N_MOD_ROWS, N_ADA, D_MODEL)


def _mod_spec(is_ctx):
    if is_ctx:
        return pl.BlockSpec((1, N_ADA, D_MODEL), lambda j, b: (CTX_MOD_ROW, 0, 0))
    return pl.BlockSpec((1, N_ADA, D_MODEL), lambda j, b: (b, 0, 0))


def _proj_kernel(x_ref, mod_ref, g_ref, win_ref, qn_ref, kvn_ref, wuq_ref, wuk_ref, wuv_ref, *rest, rotary):
    if rotary:
        rope_ref, rest = rest[0], rest[1:]
    mq_ref, mk_ref, mv_ref, dq_ref, dk_ref, dv_ref, wq_ref, wk_ref, wv_ref, nq_ref, nk_ref, nv_ref = rest
    mod = mod_ref[0]
    h = _adaln(x_ref[0], g_ref[...], mod[0:1], mod[1:2]).astype(BF16)
    lane = _lane_iota(h.shape[0])

    def seg(start, width):
        return _dot(h, win_ref[:, start:start + width])

    def rope(x, base, quarter):
        if not rotary:
            return x
        return (x * rope_ref[base]
                + pltpu.roll(x, LANES - quarter, 1) * rope_ref[base + 1]
                + pltpu.roll(x, quarter, 1) * rope_ref[base + 2])

    def put(ref, val, rope_base=None, quarter=None, scale=None):
        for i in range(val.shape[1] // LANES):
            blk = val[:, _blk(i)]
            if rope_base is not None:
                blk = rope(blk, rope_base, quarter)
            if scale is not None:
                blk = blk * scale
            ref[0, :, _blk(i)] = blk.astype(BF16)

    def put_values(ref, val):
        for hd in range(val.shape[1] // HALF):
            blk = val[:, _blk(hd // 2)]
            if hd % 2:
                blk = pltpu.roll(blk, HALF, 1)
            ref[0, :, _blk(hd)] = jnp.where(lane < HALF, blk, 1.0).astype(BF16)

    cq = _rms(seg(P_CQ, MLA_Q_RANK), qn_ref[...]).astype(BF16)
    put(mq_ref, _dot(cq, wuq_ref[...]), 0, MLA_ROPE // 4, MLA_SCALE * LOG2E)
    ckv = _rms(seg(P_CKV, MLA_KV_RANK), kvn_ref[...]).astype(BF16)
    kr = rope(seg(P_KR, LANES), 0, MLA_ROPE // 4)
    kn = _dot(ckv, wuk_ref[...])
    for i in range(MLA_HEADS):
        mk_ref[0, :, _blk(i)] = (kn[:, _blk(i)] + kr).astype(BF16)
    put_values(mv_ref, _dot(ckv, wuv_ref[...]))
    put(dq_ref, seg(P_DQ, 256), 3, DIFF_QK // 4, DIFF_SCALE * LOG2E)
    put(dk_ref, seg(P_DK, 256), 3, DIFF_QK // 4)
    put_values(dv_ref, seg(P_DV, 256))
    put(wq_ref, seg(P_WQ, 256), 6, WIN_DIM // 4, WIN_SCALE * LOG2E)
    put(wk_ref, seg(P_WK, 128), 6, WIN_DIM // 4)
    put_values(wv_ref, seg(P_WV, 128))
    put(nq_ref, seg(P_NQ, 256), scale=NA_SCALE * LOG2E)
    put(nk_ref, seg(P_NK, 256))
    put_values(nv_ref, seg(P_NV, 256))


_PROJ_OUT_WIDTHS = (512, 512, 512, 256, 256, 512, 256, 128, 256, 256, 256, 512)


def _project(xs, mods, g, w_in_p, q_norm, kv_norm, w_uq_p, w_uk_p, w_uv, rope_tab, is_ctx):
    n_tok = xs.shape[1]
    tm = min(TM_PROJ, n_tok)
    tok = lambda width: pl.BlockSpec((1, tm, width), lambda j, b: (b, j, 0))
    in_specs = [
        tok(D_MODEL),
        _mod_spec(is_ctx),
        _const_spec((1, D_MODEL)),
        _const_spec((D_MODEL, P_WIDTH)),
        _const_spec((1, MLA_Q_RANK)),
        _const_spec((1, MLA_KV_RANK)),
        _const_spec((MLA_Q_RANK, MLA_HEADS * LANES)),
        _const_spec((MLA_KV_RANK, MLA_HEADS * LANES)),
        _const_spec((MLA_KV_RANK, MLA_HEADS * MLA_V)),
    ]
    args = [xs, mods, g, w_in_p, q_norm, kv_norm, w_uq_p, w_uk_p, w_uv]
    if not is_ctx:
        in_specs.append(pl.BlockSpec((9, tm, LANES), lambda j, b: (0, j, 0)))
        args.append(rope_tab)
    return pl.pallas_call(
        functools.partial(_proj_kernel, rotary=not is_ctx),
        grid=(n_tok // tm, BATCH),
        in_specs=in_specs,
        out_specs=[tok(w) for w in _PROJ_OUT_WIDTHS],
        out_shape=[jax.ShapeDtypeStruct((BATCH, n_tok, w), BF16) for w in _PROJ_OUT_WIDTHS],
        compiler_params=_params(2),
        name="project_ctx" if is_ctx else "project",
    )(*args)


def _attend(q, ks, vs, lane, post=None, sink=None):
    ss = []
    for i, k in enumerate(ks):
        s = _dot_nt(q, k)
        if post is not None and post[i] is not None:
            s = post[i](s)
        ss.append(s)
    m = ss[0].max(axis=-1, keepdims=True)
    for s in ss[1:]:
        m = jnp.maximum(m, s.max(axis=-1, keepdims=True))
    if sink is not None:
        m = jnp.maximum(m, sink)
    acc = None
    for s, v in zip(ss, vs):
        part = _dot(jnp.exp2(s - m).astype(BF16), v)
        acc = part if acc is None else acc + part
    if sink is not None:
        acc = acc + jnp.where(lane >= HALF, jnp.exp2(sink - m), 0.0)
    return acc


def _normalise(acc):
    return acc * (1.0 / pltpu.roll(acc, HALF, 1))


def _pair(lane, even, odd):
    return jnp.where(lane < HALF, even, pltpu.roll(odd, HALF, 1))


def _half_mask(lane, q, start, width):
    return jnp.where((lane >= start) & (lane < start + width), q, jnp.zeros_like(q))


def _mla_heads(q_ref, kv_parts, o_ref, lane):
    for blk in range(MLA_HEADS // 2):
        outs = []
        for hd in (2 * blk, 2 * blk + 1):
            acc = _attend(q_ref[0, :, _blk(hd)], [k[0, :, _blk(hd)] for k, _ in kv_parts],
                          [v[0, :, _blk(hd)] for _, v in kv_parts], lane)
            outs.append(_normalise(acc))
        o_ref[0, :, _blk(blk)] = _pair(lane, *outs).astype(BF16)


def _diff_lambda(lam_ref, lam_init):
    lp = lam_ref[...]
    return (jnp.exp(jnp.sum(lp[0:1] * lp[1:2], axis=-1, keepdims=True))
            - jnp.exp(jnp.sum(lp[2:3] * lp[3:4], axis=-1, keepdims=True)) + lam_init)


def _diff_heads(q_ref, kv_parts, lam_ref, sub_ref, o_ref, lane, lam_init):
    lam = _diff_lambda(lam_ref, lam_init)
    for blk in range(DIFF_HEADS // 2):
        qb = q_ref[0, :, _blk(blk)]
        ks = [k[0, :, _blk(blk)] for k, _ in kv_parts]
        outs = []
        for u in range(2):
            vs = [v[0, :, _blk(2 * blk + u)] for _, v in kv_parts]
            a1 = _attend(_half_mask(lane, qb, HALF * u, DIFF_QK), ks, vs, lane)
            a2 = _attend(_half_mask(lane, qb, HALF * u + DIFF_QK, DIFF_QK), ks, vs, lane)
            outs.append(_normalise(a1) - lam * _normalise(a2))
        pair = _pair(lane, *outs)
        sq = pair * pair
        lo = jnp.sum(jnp.where(lane < HALF, sq, 0.0), axis=-1, keepdims=True)
        hi = jnp.sum(jnp.where(lane < HALF, 0.0, sq), axis=-1, keepdims=True)
        ms = jnp.where(lane < HALF, lo, hi) * (1.0 / DIFF_V)
        o_ref[0, :, _blk(blk)] = (pair * lax.rsqrt(ms + EPS) * sub_ref[...] * (1.0 - lam_init)).astype(BF16)


def _win_heads(q_ref, k_parts, v_parts, posts, sink_ref, o_ref, lane):
    for blk in range(2):
        qb = q_ref[0, :, _blk(blk)]
        outs = []
        for u in range(WIN_KV_HEADS):
            acc = _attend(_half_mask(lane, qb, HALF * u, HALF), k_parts, [v[:, _blk(u)] for v in v_parts], lane,
                          post=posts, sink=sink_ref[blk + 2 * u] * LOG2E)
            outs.append(_normalise(acc))
        o_ref[0, :, _blk(blk)] = _pair(lane, *outs).astype(BF16)


def _na_heads(q_ref, kv_parts, posts_of_head, o_ref, lane):
    for blk in range(NA_HEADS // 2):
        qb = q_ref[0, :, _blk(blk)]
        outs = []
        for u in range(2):
            hd = 2 * blk + u
            acc = _attend(_half_mask(lane, qb, HALF * u, HALF), [k(_blk(blk)) for k, _ in kv_parts],
                          [v(_blk(hd)) for _, v in kv_parts], lane, post=posts_of_head(hd))
            outs.append(_normalise(acc))
        o_ref[0, :, _blk(blk)] = _pair(lane, *outs).astype(BF16)


def _win_tile(sink_ref, q_ref, kl_ref, vl_ref, kc_ref, vc_ref, o_ref, j):
    start = pl.multiple_of(jnp.clip(j * TQ - WINDOW, 0, SEQ - WIN_N_LOC), WINDOW)
    qpos = j * TQ + lax.broadcasted_iota(jnp.int32, (TQ, WIN_N_LOC), 0)
    kpos = start + lax.broadcasted_iota(jnp.int32, (TQ, WIN_N_LOC), 1)
    allowed = jnp.abs(qpos - kpos) <= WINDOW
    band = lambda s: jnp.where(allowed, s, NEG_INF)
    _win_heads(q_ref, [kl_ref[0, pl.ds(start, WIN_N_LOC), :], kc_ref[0]],
               [vl_ref[0, pl.ds(start, WIN_N_LOC), :], vc_ref[0]], [band, None], sink_ref, o_ref, _lane_iota(TQ))


def _na_tile(q_ref, kl_ref, vl_ref, kc_ref, vc_ref, bias_ref, o_ref, j):
    n_strips = GRID_ROWS // NA_STRIP
    first = jnp.clip(j * NA_STRIP - NA_ROWS // 2, 0, GRID_ROWS - NA_KEY_ROWS)
    start = pl.multiple_of(first * GRID_W, GRID_W)
    variant = (j >= 1).astype(jnp.int32) + (j >= n_strips - 1).astype(jnp.int32)
    parts = [(lambda ls: kl_ref[0, pl.ds(start, NA_N_LOC), ls], lambda ls: vl_ref[0, pl.ds(start, NA_N_LOC), ls]),
             (lambda ls: kc_ref[0, :, ls], lambda ls: vc_ref[0, :, ls])]
    posts = lambda hd: [lambda s: s + bias_ref[variant, hd], None]
    _na_heads(q_ref, parts, posts, o_ref, _lane_iota(TQ))


def _attention_kernel(sink_ref, mq, mkl, mvl, mkc, mvc, dq, dkl, dvl, dkc, dvc, wq, wkl, wvl, wkc, wvc,
                      nq, nkl, nvl, nkc, nvc, lam_ref, sub_ref, bias_ref, oa_ref, ob_ref, oc_ref, od_ref,
                      *, lam_init):
    lane = _lane_iota(TQ_DENSE)
    _mla_heads(mq, [(mkl, mvl), (mkc, mvc)], oa_ref, lane)
    _diff_heads(dq, [(dkl, dvl), (dkc, dvc)], lam_ref, sub_ref, ob_ref, lane, lam_init)
    for t in range(TQ_DENSE // TQ):
        rows = pl.ds(t * TQ, TQ)
        j = pl.program_id(1) * (TQ_DENSE // TQ) + t
        _win_tile(sink_ref, wq.at[:, rows], wkl, wvl, wkc, wvc, oc_ref.at[:, rows], j)
        _na_tile(nq.at[:, rows], nkl, nvl, nkc, nvc, bias_ref, od_ref.at[:, rows], j)


def _qkv_specs(q, kl, vl):
    qw, kw, vw = q.shape[-1], kl.shape[-1], vl.shape[-1]
    return [pl.BlockSpec((1, TQ_DENSE, qw), lambda b, j: (b, j, 0)),
            pl.BlockSpec((1, SEQ, kw), lambda b, j: (b, 0, 0)),
            pl.BlockSpec((1, SEQ, vw), lambda b, j: (b, 0, 0)),
            pl.BlockSpec((1, CTX_LEN, kw), lambda b, j: (b, 0, 0)),
            pl.BlockSpec((1, CTX_LEN, vw), lambda b, j: (b, 0, 0))]


def _latent_attention(sink, branches, lam, sub, bias, lam_init):
    out_spec = pl.BlockSpec((1, TQ_DENSE, BRANCH_W), lambda b, j: (b, j, 0))
    return pl.pallas_call(
        functools.partial(_attention_kernel, lam_init=lam_init),
        grid=(BATCH, SEQ // TQ_DENSE),
        in_specs=[pl.BlockSpec(memory_space=pltpu.SMEM)] + [s for br in branches for s in _qkv_specs(*br[:3])]
        + [_const_spec((4, DIFF_QK)), _const_spec((1, LANES)), _const_spec(bias.shape)],
        out_specs=[out_spec] * N_BRANCH,
        out_shape=[jax.ShapeDtypeStruct((BATCH, SEQ, BRANCH_W), BF16)] * N_BRANCH,
        compiler_params=_params(2),
        name="latent_attention",
    )(sink, *[a for br in branches for a in br], lam, sub, bias)


def _ctx_kernel(sink_ref, mq, mk, mv, dq, dk, dv, wq, wk, wv, nq, nk, nv, lam_ref, sub_ref,
                oa_ref, ob_ref, oc_ref, od_ref, *, lam_init):
    lane = _lane_iota(CTX_LEN)
    _mla_heads(mq, [(mk, mv)], oa_ref, lane)
    _diff_heads(dq, [(dk, dv)], lam_ref, sub_ref, ob_ref, lane, lam_init)
    _win_heads(wq, [wk[0]], [wv[0]], None, sink_ref, oc_ref, lane)
    _na_heads(nq, [(lambda ls: nk[0, :, ls], lambda ls: nv[0, :, ls])], lambda hd: None, od_ref, lane)


def _ctx_attention(proj_ctx, sink, lam, sub, lam_init):
    tok = lambda width: pl.BlockSpec((1, CTX_LEN, width), lambda b: (b, 0, 0))
    return pl.pallas_call(
        functools.partial(_ctx_kernel, lam_init=lam_init),
        grid=(BATCH,),
        in_specs=[pl.BlockSpec(memory_space=pltpu.SMEM)] + [tok(w) for w in _PROJ_OUT_WIDTHS]
        + [_const_spec((4, DIFF_QK)), _const_spec((1, LANES))],
        out_specs=[tok(BRANCH_W)] * N_BRANCH,
        out_shape=[jax.ShapeDtypeStruct((BATCH, CTX_LEN, BRANCH_W), BF16)] * N_BRANCH,
        compiler_params=_params(1),
        name="ctx_attention",
    )(sink, *proj_ctx, lam, sub)


def _channel_kernel(x_ref, mod_ref, gm_ref, oa_ref, ob_ref, oc_ref, od_ref, wg_ref, bg_ref, wb_ref, wo_ref,
                    gf_ref, w1_ref, w2_ref, fn_ref, out_ref, *, final):
    x = x_ref[0]
    mod = mod_ref[0]
    h = _adaln(x, gm_ref[...], mod[0:1], mod[1:2]).astype(BF16)
    y = None
    for i, o_ref in enumerate((oa_ref, ob_ref, oc_ref, od_ref)):
        cols = slice(D_MODEL * i, D_MODEL * (i + 1))
        gate = jax.nn.sigmoid(_dot(h, wg_ref[:, cols]) + bg_ref[:, cols])
        term = gate * _dot(o_ref[0], wb_ref[BRANCH_W * i:BRANCH_W * (i + 1), :])
        y = term if y is None else y + term
    x = x + mod[2:3] * _dot(y.astype(BF16), wo_ref[...])
    h = _adaln(x, gf_ref[...], mod[3:4], mod[4:5]).astype(BF16)
    acc = None
    for i in range(FFN_HIDDEN // D_MODEL):
        cols = slice(D_MODEL * i, D_MODEL * (i + 1))
        u = jnp.square(jnp.maximum(_dot(h, w1_ref[:, cols]), 0.0)).astype(BF16)
        part = _dot(u, w2_ref[cols, :])
        acc = part if acc is None else acc + part
    out = x + mod[5:6] * acc
    out_ref[0] = _rms(out, fn_ref[...]) if final else out


def _channel_mix(xs, mods, outs, g_mix, w_gate, b_gate, w_branch, w_out, g_ffn, w1, w2, final_norm, is_ctx, final):
    tm = CTX_LEN if is_ctx else TM_LAT
    tok = lambda width: pl.BlockSpec((1, tm, width), lambda j, b: (b, j, 0))
    return pl.pallas_call(
        functools.partial(_channel_kernel, final=final),
        grid=(xs.shape[1] // tm, BATCH),
        in_specs=[tok(D_MODEL), _mod_spec(is_ctx), _const_spec((1, D_MODEL))] + [tok(BRANCH_W)] * N_BRANCH + [
            _const_spec((D_MODEL, N_BRANCH * D_MODEL)),
            _const_spec((1, N_BRANCH * D_MODEL)),
            _const_spec((N_BRANCH * BRANCH_W, D_MODEL)),
            _const_spec((D_MODEL, D_MODEL)),
            _const_spec((1, D_MODEL)),
            _const_spec((D_MODEL, FFN_HIDDEN)),
            _const_spec((FFN_HIDDEN, D_MODEL)),
            _const_spec((1, D_MODEL)),
        ],
        out_specs=tok(D_MODEL),
        out_shape=jax.ShapeDtypeStruct(xs.shape, F32),
        compiler_params=_params(2),
        name="channel_mix_ctx" if is_ctx else ("channel_mix_final" if final else "channel_mix"),
    )(xs, mods, g_mix, *outs, w_gate, b_gate, w_branch, w_out, g_ffn, w1, w2, final_norm)


def _rope_tables():
    t = jnp.arange(SEQ, dtype=jnp.int32)
    row = (t // GRID_W).astype(F32)[:, None]
    col = (t % GRID_W).astype(F32)[:, None]

    def table(dim, lane_start, lane_stop):
        quarter = dim // 4
        inv_freq = ROPE_BASE ** (-jnp.arange(quarter, dtype=F32) / quarter)
        ang = jnp.concatenate([row * inv_freq, row * inv_freq, col * inv_freq, col * inv_freq], axis=-1)
        reps = (lane_stop - lane_start) // dim
        cos = jnp.tile(jnp.cos(ang), (1, reps))
        sin = jnp.tile(jnp.sin(ang), (1, reps))
        lower = (np.arange(lane_stop - lane_start) % (2 * quarter)) < quarter
        pad = lambda a, fill: jnp.pad(a, ((0, 0), (lane_start, LANES - lane_stop)), constant_values=fill)
        return [pad(cos, 1.0), pad(jnp.where(lower, -sin, 0.0), 0.0), pad(jnp.where(lower, 0.0, sin), 0.0)]

    tabs = table(MLA_ROPE, MLA_NOPE, MLA_NOPE + MLA_ROPE) + table(DIFF_QK, 0, LANES) + table(WIN_DIM, 0, LANES)
    return jnp.stack(tabs)


def _na_bias_tables(rpb):
    n_dr = 2 * NA_ROWS - 1
    period = 2 * GRID_W - 1
    pad = GRID_W - NA_COLS
    rp = jnp.pad(rpb.astype(F32), ((0, 0), (0, 0), (pad, pad)))
    flat = jnp.tile(rp, (1, 1, GRID_W + 1))[:, :, :GRID_W * (period + 1)]
    by_col = flat.reshape(NA_HEADS, n_dr, GRID_W, period + 1)[:, :, ::-1, :GRID_W]
    by_query = jnp.pad(jnp.transpose(by_col, (0, 2, 1, 3)), ((0, 0), (0, 0), (NA_STRIP, NA_STRIP), (0, 0)))
    by_query = by_query.reshape(NA_HEADS, GRID_W, (n_dr + 2 * NA_STRIP) * GRID_W)
    qc = np.arange(GRID_W)[:, None]
    kc = np.arange(GRID_W)[None, :]
    c_start = np.clip(qc - NA_COLS // 2, 0, GRID_W - NA_COLS)
    in_cols = (kc >= c_start) & (kc < c_start + NA_COLS)
    n_strips = GRID_ROWS // NA_STRIP
    tables, masks = [], []
    for strip in (0, 1, n_strips - 1):
        first_key_row = int(np.clip(strip * NA_STRIP - NA_ROWS // 2, 0, GRID_ROWS - NA_KEY_ROWS))
        offset = first_key_row - strip * NA_STRIP + NA_ROWS - 1
        first_lane = lambda dr: (offset - dr + NA_STRIP) * GRID_W
        tables.append(jnp.stack(
            [by_query[:, :, first_lane(dr):first_lane(dr) + NA_N_LOC] for dr in range(NA_STRIP)],
            axis=1).reshape(NA_HEADS, TQ, NA_N_LOC))
        r = strip * NA_STRIP + np.arange(NA_STRIP)[:, None]
        r_start = np.clip(r - NA_ROWS // 2, 0, GRID_ROWS - NA_ROWS)
        key_row = first_key_row + np.arange(NA_KEY_ROWS)[None, :]
        in_rows = (key_row >= r_start) & (key_row < r_start + NA_ROWS)
        masks.append((in_rows[:, None, :, None] & in_cols[None, :, None, :]).reshape(TQ, NA_N_LOC))
    return jnp.where(np.stack(masks)[:, None], jnp.stack(tables), NEG_INF) * LOG2E


def _pad_heads(w, heads, width, keep):
    w = w.reshape(w.shape[0], heads, width)[:, :, keep[0]:keep[1]]
    return jnp.pad(w, ((0, 0), (0, 0), (0, LANES - (keep[1] - keep[0])))).reshape(w.shape[0], heads * LANES)


_WIN_HEAD_ORDER = (0, 2, 1, 3)


def _prep_w_in(w):
    seg = lambda a, b: w[:, a:b]
    kr = jnp.pad(seg(384, 416), ((0, 0), (MLA_NOPE, LANES - MLA_NOPE - MLA_ROPE)))
    wq = seg(1184, 1440).reshape(D_MODEL, WIN_HEADS, WIN_DIM)[:, np.array(_WIN_HEAD_ORDER)].reshape(D_MODEL, 256)
    return jnp.concatenate([seg(0, 384), kr, seg(416, 1184), wq, seg(1440, 2464)], axis=1).astype(BF16)


def _prep_w_branch(w):
    win = w[512:768].reshape(WIN_HEADS, WIN_DIM, D_MODEL)[np.array(_WIN_HEAD_ORDER)].reshape(256, D_MODEL)
    return jnp.concatenate([w[:512], win, w[768:]], axis=0).astype(BF16)


def _lambda_init(layer):
    return 0.8 - 0.6 * math.exp(-0.3 * layer)


def kernel(x, c, ctx, c_ctx, w_ada, b_ada, norm_mix, norm_ffn, w_in, mla_q_norm, mla_kv_norm, w_uq, w_ukv,
           diff_lambda, diff_subln, win_sink, na_rpb, w_gate, b_gate, w_branch, w_out, w_ff1, w_ff2, final_norm):
    cvec = jnp.concatenate([c, c_ctx[None, :], jnp.zeros((N_MOD_ROWS - BATCH - 1, D_MODEL), F32)], axis=0)
    mods = _modulation(cvec, w_ada, b_ada)
    rope_tab = _rope_tables()
    na_bias = jax.vmap(_na_bias_tables)(na_rpb)
    row = lambda v: v.reshape(1, -1)
    xc = ctx
    for l in range(DEPTH):
        need_ctx = l < DEPTH - 1
        lam_init = _lambda_init(l)
        ukv = w_ukv[l].reshape(MLA_KV_RANK, MLA_HEADS, MLA_NOPE + MLA_V)
        proj_w = (row(norm_mix[l]), _prep_w_in(w_in[l]), row(mla_q_norm[l]), row(mla_kv_norm[l]),
                  _pad_heads(w_uq[l], MLA_HEADS, MLA_NOPE + MLA_ROPE, (0, MLA_NOPE + MLA_ROPE)).astype(BF16),
                  _pad_heads(w_ukv[l], MLA_HEADS, MLA_NOPE + MLA_V, (0, MLA_NOPE)).astype(BF16),
                  ukv[:, :, MLA_NOPE:].reshape(MLA_KV_RANK, MLA_HEADS * MLA_V).astype(BF16))
        mq, mk, mv, dq, dk, dv, wq, wk, wv, nq, nk, nv = _project(x, mods[l], *proj_w, rope_tab, is_ctx=False)
        pc = _project(xc, mods[l], *proj_w, None, is_ctx=True)
        sub = jnp.tile(diff_subln[l], 2).reshape(1, LANES)
        branches = ((mq, mk, mv, pc[1], pc[2]), (dq, dk, dv, pc[4], pc[5]),
                    (wq, wk, wv, pc[7], pc[8]), (nq, nk, nv, pc[10], pc[11]))
        outs = _latent_attention(win_sink[l], branches, diff_lambda[l], sub, na_bias[l], lam_init)
        mix_w = (row(norm_mix[l]), w_gate[l].astype(BF16), row(b_gate[l]), _prep_w_branch(w_branch[l]),
                 w_out[l].astype(BF16), row(norm_ffn[l]), w_ff1[l].astype(BF16), w_ff2[l].astype(BF16),
                 row(final_norm))
        x = _channel_mix(x, mods[l], outs, *mix_w, is_ctx=False, final=not need_ctx)
        if need_ctx:
            outs_c = _ctx_attention(pc, win_sink[l], diff_lambda[l], sub, lam_init)
            xc = _channel_mix(xc, mods[l], outs_c, *mix_w, is_ctx=True, final=False)
    return x
```

```python
import functools
import math

import numpy as np
import jax
import jax.numpy as jnp
from jax import lax
from jax.experimental import pallas as pl
from jax.experimental.pallas import tpu as pltpu

D_MODEL = 1024
BATCH = 8
SEQ = 2048
DEPTH = 2
CTX_LEN = 256
GRID_W = 64
GRID_ROWS = SEQ // GRID_W
ROPE_BASE = 10000.0
NEG_INF = -1e30
EPS = 1e-6
N_ADA = 6
LOG2E = math.log2(math.e)

MLA_HEADS = 4
MLA_NOPE = 64
MLA_ROPE = 32
MLA_V = 64
MLA_Q_RANK = 256
MLA_KV_RANK = 128
MLA_SCALE = (MLA_NOPE + MLA_ROPE) ** -0.5
DIFF_HEADS = 4
DIFF_QK = 32
DIFF_V = 64
DIFF_SCALE = DIFF_QK ** -0.5
WIN_HEADS = 4
WIN_KV_HEADS = 2
WIN_DIM = 64
WINDOW = 128
WIN_SCALE = WIN_DIM ** -0.5
NA_HEADS = 4
NA_DIM = 64
NA_ROWS = 8
NA_COLS = 16
NA_SCALE = NA_DIM ** -0.5
FFN_HIDDEN = 4 * D_MODEL
N_BRANCH = 4
BRANCH_W = 256

LANES = 128
HALF = LANES // 2
VMEM_LIMIT = 56 * 1024 * 1024

BF16 = jnp.bfloat16
F32 = jnp.float32

P_CQ, P_CKV, P_KR = 0, 256, 384
P_DQ, P_DK, P_DV = 512, 768, 1024
P_WQ, P_WK, P_WV = 1280, 1536, 1664
P_NQ, P_NK, P_NV = 1792, 2048, 2304
P_WIDTH = 2560

TM_PROJ = 512
TQ = 256
TQ_DENSE = 512
TM_LAT = 512
N_MOD_ROWS = 16
CTX_MOD_ROW = BATCH

NA_STRIP = TQ // GRID_W
NA_KEY_ROWS = 12
NA_N_LOC = NA_KEY_ROWS * GRID_W
NA_BIAS_VARIANTS = 3
WIN_N_LOC = TQ + 2 * WINDOW


def _const_spec(shape):
    nd = len(shape)
    return pl.BlockSpec(shape, lambda *_: (0,) * nd, pipeline_mode=pl.Buffered(1))


def _params(n_axes):
    return pltpu.CompilerParams(dimension_semantics=("arbitrary",) * n_axes, vmem_limit_bytes=VMEM_LIMIT)


def _rms(x, g):
    return x * lax.rsqrt(jnp.mean(x * x, axis=-1, keepdims=True) + EPS) * g


def _adaln(x, g, shift, scale):
    return _rms(x, g) * (1.0 + scale) + shift


def _dot(a, b):
    return jnp.dot(a, b, preferred_element_type=F32)


def _dot_nt(a, b):
    return lax.dot_general(a, b, (((1,), (1,)), ((), ())), preferred_element_type=F32)


def _lane_iota(rows):
    return lax.broadcasted_iota(jnp.int32, (rows, LANES), 1)


def _blk(i):
    return slice(LANES * i, LANES * (i + 1))


def _mod_kernel(c_ref, w_ref, b_ref, o_ref):
    c = c_ref[...]
    act = (c * jax.nn.sigmoid(c)).astype(BF16)
    o_ref[0] = _dot(act, w_ref[0].astype(BF16)) + b_ref[0]


def _modulation(cvec, w_ada, b_ada):
    tn = 3072
    n_ada = N_ADA * D_MODEL
    out = pl.pallas_call(
        _mod_kernel,
        grid=(DEPTH, n_ada // tn),
        in_specs=[
            pl.BlockSpec((N_MOD_ROWS, D_MODEL), lambda l, n: (0, 0)),
            pl.BlockSpec((1, D_MODEL, tn), lambda l, n: (l, 0, n)),
            pl.BlockSpec((1, 1, tn), lambda l, n: (l, 0, n)),
        ],
        out_specs=pl.BlockSpec((1, N_MOD_ROWS, tn), lambda l, n: (l, 0, n)),
        out_shape=jax.ShapeDtypeStruct((DEPTH, N_MOD_ROWS, n_ada), F32),
        compiler_params=_params(2),
        name="modulation",
    )(cvec, w_ada, b_ada.reshape(DEPTH, 1, n_ada))
    return out.reshape(DEPTH, N_MOD_ROWS, N_ADA, D_MODEL)


def _mod_spec(is_ctx):
    if is_ctx:
        return pl.BlockSpec((1, N_ADA, D_MODEL), lambda j, b: (CTX_MOD_ROW, 0, 0))
    return pl.BlockSpec((1, N_ADA, D_MODEL), lambda j, b: (b, 0, 0))


def _proj_kernel(x_ref, mod_ref, g_ref, win_ref, qn_ref, kvn_ref, wuq_ref, wuk_ref, wuv_ref, *rest, rotary):
    if rotary:
        rope_ref, rest = rest[0], rest[1:]
    mq_ref, mk_ref, mv_ref, dq_ref, dk_ref, dv_ref, wq_ref, wk_ref, wv_ref, nq_ref, nk_ref, nv_ref = rest
    mod = mod_ref[0]
    h = _adaln(x_ref[0], g_ref[...], mod[0:1], mod[1:2]).astype(BF16)
    lane = _lane_iota(h.shape[0])

    def seg(start, width):
        return _dot(h, win_ref[:, start:start + width])

    def rope(x, base, quarter):
        if not rotary:
            return x
        return (x * rope_ref[base]
                + pltpu.roll(x, LANES - quarter, 1) * rope_ref[base + 1]
                + pltpu.roll(x, quarter, 1) * rope_ref[base + 2])

    def put(ref, val, rope_base=None, quarter=None, scale=None):
        for i in range(val.shape[1] // LANES):
            blk = val[:, _blk(i)]
            if rope_base is not None:
                blk = rope(blk, rope_base, quarter)
            if scale is not None:
                blk = blk * scale
            ref[0, :, _blk(i)] = blk.astype(BF16)

    def put_values(ref, val):
        for hd in range(val.shape[1] // HALF):
            blk = val[:, _blk(hd // 2)]
            if hd % 2:
                blk = pltpu.roll(blk, HALF, 1)
            ref[0, :, _blk(hd)] = jnp.where(lane < HALF, blk, 1.0).astype(BF16)

    cq = _rms(seg(P_CQ, MLA_Q_RANK), qn_ref[...]).astype(BF16)
    put(mq_ref, _dot(cq, wuq_ref[...]), 0, MLA_ROPE // 4, MLA_SCALE * LOG2E)
    ckv = _rms(seg(P_CKV, MLA_KV_RANK), kvn_ref[...]).astype(BF16)
    kr = rope(seg(P_KR, LANES), 0, MLA_ROPE // 4)
    kn = _dot(ckv, wuk_ref[...])
    for i in range(MLA_HEADS):
        mk_ref[0, :, _blk(i)] = (kn[:, _blk(i)] + kr).astype(BF16)
    put_values(mv_ref, _dot(ckv, wuv_ref[...]))
    put(dq_ref, seg(P_DQ, 256), 3, DIFF_QK // 4, DIFF_SCALE * LOG2E)
    put(dk_ref, seg(P_DK, 256), 3, DIFF_QK // 4)
    put_values(dv_ref, seg(P_DV, 256))
    put(wq_ref, seg(P_WQ, 256), 6, WIN_DIM // 4, WIN_SCALE * LOG2E)
    put(wk_ref, seg(P_WK, 128), 6, WIN_DIM // 4)
    put_values(wv_ref, seg(P_WV, 128))
    put(nq_ref, seg(P_NQ, 256), scale=NA_SCALE * LOG2E)
    put(nk_ref, seg(P_NK, 256))
    put_values(nv_ref, seg(P_NV, 256))


_PROJ_OUT_WIDTHS = (512, 512, 512, 256, 256, 512, 256, 128, 256, 256, 256, 512)


def _project(xs, mods, g, w_in_p, q_norm, kv_norm, w_uq_p, w_uk_p, w_uv, rope_tab, is_ctx):
    n_tok = xs.shape[1]
    tm = min(TM_PROJ, n_tok)
    tok = lambda width: pl.BlockSpec((1, tm, width), lambda j, b: (b, j, 0))
    in_specs = [
        tok(D_MODEL),
        _mod_spec(is_ctx),
        _const_spec((1, D_MODEL)),
        _const_spec((D_MODEL, P_WIDTH)),
        _const_spec((1, MLA_Q_RANK)),
        _const_spec((1, MLA_KV_RANK)),
        _const_spec((MLA_Q_RANK, MLA_HEADS * LANES)),
        _const_spec((MLA_KV_RANK, MLA_HEADS * LANES)),
        _const_spec((MLA_KV_RANK, MLA_HEADS * MLA_V)),
    ]
    args = [xs, mods, g, w_in_p, q_norm, kv_norm, w_uq_p, w_uk_p, w_uv]
    if not is_ctx:
        in_specs.append(pl.BlockSpec((9, tm, LANES), lambda j, b: (0, j, 0)))
        args.append(rope_tab)
    return pl.pallas_call(
        functools.partial(_proj_kernel, rotary=not is_ctx),
        grid=(n_tok // tm, BATCH),
        in_specs=in_specs,
        out_specs=[tok(w) for w in _PROJ_OUT_WIDTHS],
        out_shape=[jax.ShapeDtypeStruct((BATCH, n_tok, w), BF16) for w in _PROJ_OUT_WIDTHS],
        compiler_params=_params(2),
        name="project_ctx" if is_ctx else "project",
    )(*args)


def _attend(q, ks, vs, lane, post=None, sink=None):
    ss = []
    for i, k in enumerate(ks):
        s = _dot_nt(q, k)
        if post is not None and post[i] is not None:
            s = post[i](s)
        ss.append(s)
    m = ss[0].max(axis=-1, keepdims=True)
    for s in ss[1:]:
        m = jnp.maximum(m, s.max(axis=-1, keepdims=True))
    if sink is not None:
        m = jnp.maximum(m, sink)
    acc = None
    for s, v in zip(ss, vs):
        part = _dot(jnp.exp2(s - m).astype(BF16), v)
        acc = part if acc is None else acc + part
    if sink is not None:
        acc = acc + jnp.where(lane >= HALF, jnp.exp2(sink - m), 0.0)
    return acc


def _normalise(acc):
    return acc * (1.0 / pltpu.roll(acc, HALF, 1))


def _pair(lane, even, odd):
    return jnp.where(lane < HALF, even, pltpu.roll(odd, HALF, 1))


def _half_mask(lane, q, start, width):
    return jnp.where((lane >= start) & (lane < start + width), q, jnp.zeros_like(q))


def _mla_heads(q_ref, kv_parts, o_ref, lane):
    for blk in range(MLA_HEADS // 2):
        outs = []
        for hd in (2 * blk, 2 * blk + 1):
            acc = _attend(q_ref[0, :, _blk(hd)], [k[0, :, _blk(hd)] for k, _ in kv_parts],
                          [v[0, :, _blk(hd)] for _, v in kv_parts], lane)
            outs.append(_normalise(acc))
        o_ref[0, :, _blk(blk)] = _pair(lane, *outs).astype(BF16)


def _diff_lambda(lam_ref, lam_init):
    lp = lam_ref[...]
    return (jnp.exp(jnp.sum(lp[0:1] * lp[1:2], axis=-1, keepdims=True))
            - jnp.exp(jnp.sum(lp[2:3] * lp[3:4], axis=-1, keepdims=True)) + lam_init)


def _diff_heads(q_ref, kv_parts, lam_ref, sub_ref, o_ref, lane, lam_init):
    lam = _diff_lambda(lam_ref, lam_init)
    for blk in range(DIFF_HEADS // 2):
        qb = q_ref[0, :, _blk(blk)]
        ks = [k[0, :, _blk(blk)] for k, _ in kv_parts]
        outs = []
        for u in range(2):
            vs = [v[0, :, _blk(2 * blk + u)] for _, v in kv_parts]
            a1 = _attend(_half_mask(lane, qb, HALF * u, DIFF_QK), ks, vs, lane)
            a2 = _attend(_half_mask(lane, qb, HALF * u + DIFF_QK, DIFF_QK), ks, vs, lane)
            outs.append(_normalise(a1) - lam * _normalise(a2))
        pair = _pair(lane, *outs)
        sq = pair * pair
        lo = jnp.sum(jnp.where(lane < HALF, sq, 0.0), axis=-1, keepdims=True)
        hi = jnp.sum(jnp.where(lane < HALF, 0.0, sq), axis=-1, keepdims=True)
        ms = jnp.where(lane < HALF, lo, hi) * (1.0 / DIFF_V)
        o_ref[0, :, _blk(blk)] = (pair * lax.rsqrt(ms + EPS) * sub_ref[...] * (1.0 - lam_init)).astype(BF16)


def _win_heads(q_ref, k_parts, v_parts, posts, sink_ref, o_ref, lane):
    for blk in range(2):
        qb = q_ref[0, :, _blk(blk)]
        outs = []
        for u in range(WIN_KV_HEADS):
            acc = _attend(_half_mask(lane, qb, HALF * u, HALF), k_parts, [v[:, _blk(u)] for v in v_parts], lane,
                          post=posts, sink=sink_ref[blk + 2 * u] * LOG2E)
            outs.append(_normalise(acc))
        o_ref[0, :, _blk(blk)] = _pair(lane, *outs).astype(BF16)


def _na_heads(q_ref, kv_parts, posts_of_head, o_ref, lane):
    for blk in range(NA_HEADS // 2):
        qb = q_ref[0, :, _blk(blk)]
        outs = []
        for u in range(2):
            hd = 2 * blk + u
            acc = _attend(_half_mask(lane, qb, HALF * u, HALF), [k(_blk(blk)) for k, _ in kv_parts],
                          [v(_blk(hd)) for _, v in kv_parts], lane, post=posts_of_head(hd))
            outs.append(_normalise(acc))
        o_ref[0, :, _blk(blk)] = _pair(lane, *outs).astype(BF16)


def _win_tile(sink_ref, q_ref, kl_ref, vl_ref, kc_ref, vc_ref, o_ref, j):
    start = pl.multiple_of(jnp.clip(j * TQ - WINDOW, 0, SEQ - WIN_N_LOC), WINDOW)
    qpos = j * TQ + lax.broadcasted_iota(jnp.int32, (TQ, WIN_N_LOC), 0)
    kpos = start + lax.broadcasted_iota(jnp.int32, (TQ, WIN_N_LOC), 1)
    allowed = jnp.abs(qpos - kpos) <= WINDOW
    band = lambda s: jnp.where(allowed, s, NEG_INF)
    _win_heads(q_ref, [kl_ref[0, pl.ds(start, WIN_N_LOC), :], kc_ref[0]],
               [vl_ref[0, pl.ds(start, WIN_N_LOC), :], vc_ref[0]], [band, None], sink_ref, o_ref, _lane_iota(TQ))


def _na_tile(q_ref, kl_ref, vl_ref, kc_ref, vc_ref, bias_ref, o_ref, j):
    n_strips = GRID_ROWS // NA_STRIP
    first = jnp.clip(j * NA_STRIP - NA_ROWS // 2, 0, GRID_ROWS - NA_KEY_ROWS)
    start = pl.multiple_of(first * GRID_W, GRID_W)
    variant = (j >= 1).astype(jnp.int32) + (j >= n_strips - 1).astype(jnp.int32)
    parts = [(lambda ls: kl_ref[0, pl.ds(start, NA_N_LOC), ls], lambda ls: vl_ref[0, pl.ds(start, NA_N_LOC), ls]),
             (lambda ls: kc_ref[0, :, ls], lambda ls: vc_ref[0, :, ls])]
    posts = lambda hd: [lambda s: s + bias_ref[variant, hd], None]
    _na_heads(q_ref, parts, posts, o_ref, _lane_iota(TQ))


def _attention_kernel(sink_ref, mq, mkl, mvl, mkc, mvc, dq, dkl, dvl, dkc, dvc, wq, wkl, wvl, wkc, wvc,
                      nq, nkl, nvl, nkc, nvc, lam_ref, sub_ref, bias_ref, oa_ref, ob_ref, oc_ref, od_ref,
                      *, lam_init):
    lane = _lane_iota(TQ_DENSE)
    _mla_heads(mq, [(mkl, mvl), (mkc, mvc)], oa_ref, lane)
    _diff_heads(dq, [(dkl, dvl), (dkc, dvc)], lam_ref, sub_ref, ob_ref, lane, lam_init)
    for t in range(TQ_DENSE // TQ):
        rows = pl.ds(t * TQ, TQ)
        j = pl.program_id(1) * (TQ_DENSE // TQ) + t
        _win_tile(sink_ref, wq.at[:, rows], wkl, wvl, wkc, wvc, oc_ref.at[:, rows], j)
        _na_tile(nq.at[:, rows], nkl, nvl, nkc, nvc, bias_ref, od_ref.at[:, rows], j)


def _qkv_specs(q, kl, vl):
    qw, kw, vw = q.shape[-1], kl.shape[-1], vl.shape[-1]
    return [pl.BlockSpec((1, TQ_DENSE, qw), lambda b, j: (b, j, 0)),
            pl.BlockSpec((1, SEQ, kw), lambda b, j: (b, 0, 0)),
            pl.BlockSpec((1, SEQ, vw), lambda b, j: (b, 0, 0)),
            pl.BlockSpec((1, CTX_LEN, kw), lambda b, j: (b, 0, 0)),
            pl.BlockSpec((1, CTX_LEN, vw), lambda b, j: (b, 0, 0))]


def _latent_attention(sink, branches, lam, sub, bias, lam_init):
    out_spec = pl.BlockSpec((1, TQ_DENSE, BRANCH_W), lambda b, j: (b, j, 0))
    return pl.pallas_call(
        functools.partial(_attention_kernel, lam_init=lam_init),
        grid=(BATCH, SEQ // TQ_DENSE),
        in_specs=[pl.BlockSpec(memory_space=pltpu.SMEM)] + [s for br in branches for s in _qkv_specs(*br[:3])]
        + [_const_spec((4, DIFF_QK)), _const_spec((1, LANES)), _const_spec(bias.shape)],
        out_specs=[out_spec] * N_BRANCH,
        out_shape=[jax.ShapeDtypeStruct((BATCH, SEQ, BRANCH_W), BF16)] * N_BRANCH,
        compiler_params=_params(2),
        name="latent_attention",
    )(sink, *[a for br in branches for a in br], lam, sub, bias)


def _ctx_kernel(sink_ref, mq, mk, mv, dq, dk, dv, wq, wk, wv, nq, nk, nv, lam_ref, sub_ref,
                oa_ref, ob_ref, oc_ref, od_ref, *, lam_init):
    lane = _lane_iota(CTX_LEN)
    _mla_heads(mq, [(mk, mv)], oa_ref, lane)
    _diff_heads(dq, [(dk, dv)], lam_ref, sub_ref, ob_ref, lane, lam_init)
    _win_heads(wq, [wk[0]], [wv[0]], None, sink_ref, oc_ref, lane)
    _na_heads(nq, [(lambda ls: nk[0, :, ls], lambda ls: nv[0, :, ls])], lambda hd: None, od_ref, lane)


def _ctx_attention(proj_ctx, sink, lam, sub, lam_init):
    tok = lambda width: pl.BlockSpec((1, CTX_LEN, width), lambda b: (b, 0, 0))
    return pl.pallas_call(
        functools.partial(_ctx_kernel, lam_init=lam_init),
        grid=(BATCH,),
        in_specs=[pl.BlockSpec(memory_space=pltpu.SMEM)] + [tok(w) for w in _PROJ_OUT_WIDTHS]
        + [_const_spec((4, DIFF_QK)), _const_spec((1, LANES))],
        out_specs=[tok(BRANCH_W)] * N_BRANCH,
        out_shape=[jax.ShapeDtypeStruct((BATCH, CTX_LEN, BRANCH_W), BF16)] * N_BRANCH,
        compiler_params=_params(1),
        name="ctx_attention",
    )(sink, *proj_ctx, lam, sub)


def _channel_kernel(x_ref, mod_ref, gm_ref, oa_ref, ob_ref, oc_ref, od_ref, wg_ref, bg_ref, wb_ref, wo_ref,
                    gf_ref, w1_ref, w2_ref, fn_ref, out_ref, *, final):
    x = x_ref[0]
    mod = mod_ref[0]
    h = _adaln(x, gm_ref[...], mod[0:1], mod[1:2]).astype(BF16)
    y = None
    for i, o_ref in enumerate((oa_ref, ob_ref, oc_ref, od_ref)):
        cols = slice(D_MODEL * i, D_MODEL * (i + 1))
        gate = jax.nn.sigmoid(_dot(h, wg_ref[:, cols]) + bg_ref[:, cols])
        term = gate * _dot(o_ref[0], wb_ref[BRANCH_W * i:BRANCH_W * (i + 1), :])
        y = term if y is None else y + term
    x = x + mod[2:3] * _dot(y.astype(BF16), wo_ref[...])
    h = _adaln(x, gf_ref[...], mod[3:4], mod[4:5]).astype(BF16)
    acc = None
    for i in range(FFN_HIDDEN // D_MODEL):
        cols = slice(D_MODEL * i, D_MODEL * (i + 1))
        u = jnp.square(jnp.maximum(_dot(h, w1_ref[:, cols]), 0.0)).astype(BF16)
        part = _dot(u, w2_ref[cols, :])
        acc = part if acc is None else acc + part
    out = x + mod[5:6] * acc
    out_ref[0] = _rms(out, fn_ref[...]) if final else out


def _channel_mix(xs, mods, outs, g_mix, w_gate, b_gate, w_branch, w_out, g_ffn, w1, w2, final_norm, is_ctx, final):
    tm = CTX_LEN if is_ctx else TM_LAT
    tok = lambda width: pl.BlockSpec((1, tm, width), lambda j, b: (b, j, 0))
    return pl.pallas_call(
        functools.partial(_channel_kernel, final=final),
        grid=(xs.shape[1] // tm, BATCH),
        in_specs=[tok(D_MODEL), _mod_spec(is_ctx), _const_spec((1, D_MODEL))] + [tok(BRANCH_W)] * N_BRANCH + [
            _const_spec((D_MODEL, N_BRANCH * D_MODEL)),
            _const_spec((1, N_BRANCH * D_MODEL)),
            _const_spec((N_BRANCH * BRANCH_W, D_MODEL)),
            _const_spec((D_MODEL, D_MODEL)),
            _const_spec((1, D_MODEL)),
            _const_spec((D_MODEL, FFN_HIDDEN)),
            _const_spec((FFN_HIDDEN, D_MODEL)),
            _const_spec((1, D_MODEL)),
        ],
        out_specs=tok(D_MODEL),
        out_shape=jax.ShapeDtypeStruct(xs.shape, F32),
        compiler_params=_params(2),
        name="channel_mix_ctx" if is_ctx else ("channel_mix_final" if final else "channel_mix"),
    )(xs, mods, g_mix, *outs, w_gate, b_gate, w_branch, w_out, g_ffn, w1, w2, final_norm)


def _rope_tables():
    t = jnp.arange(SEQ, dtype=jnp.int32)
    row = (t // GRID_W).astype(F32)[:, None]
    col = (t % GRID_W).astype(F32)[:, None]

    def table(dim, lane_start, lane_stop):
        quarter = dim // 4
        inv_freq = ROPE_BASE ** (-jnp.arange(quarter, dtype=F32) / quarter)
        ang = jnp.concatenate([row * inv_freq, row * inv_freq, col * inv_freq, col * inv_freq], axis=-1)
        reps = (lane_stop - lane_start) // dim
        cos = jnp.tile(jnp.cos(ang), (1, reps))
        sin = jnp.tile(jnp.sin(ang), (1, reps))
        lower = (np.arange(lane_stop - lane_start) % (2 * quarter)) < quarter
        pad = lambda a, fill: jnp.pad(a, ((0, 0), (lane_start, LANES - lane_stop)), constant_values=fill)
        return [pad(cos, 1.0), pad(jnp.where(lower, -sin, 0.0), 0.0), pad(jnp.where(lower, 0.0, sin), 0.0)]

    tabs = table(MLA_ROPE, MLA_NOPE, MLA_NOPE + MLA_ROPE) + table(DIFF_QK, 0, LANES) + table(WIN_DIM, 0, LANES)
    return jnp.stack(tabs)


def _na_bias_tables(rpb):
    n_dr = 2 * NA_ROWS - 1
    period = 2 * GRID_W - 1
    pad = GRID_W - NA_COLS
    rp = jnp.pad(rpb.astype(F32), ((0, 0), (0, 0), (pad, pad)))
    flat = jnp.tile(rp, (1, 1, GRID_W + 1))[:, :, :GRID_W * (period + 1)]
    by_col = flat.reshape(NA_HEADS, n_dr, GRID_W, period + 1)[:, :, ::-1, :GRID_W]
    by_query = jnp.pad(jnp.transpose(by_col, (0, 2, 1, 3)), ((0, 0), (0, 0), (NA_STRIP, NA_STRIP), (0, 0)))
    by_query = by_query.reshape(NA_HEADS, GRID_W, (n_dr + 2 * NA_STRIP) * GRID_W)
    qc = np.arange(GRID_W)[:, None]
    kc = np.arange(GRID_W)[None, :]
    c_start = np.clip(qc - NA_COLS // 2, 0, GRID_W - NA_COLS)
    in_cols = (kc >= c_start) & (kc < c_start + NA_COLS)
    n_strips = GRID_ROWS // NA_STRIP
    tables, masks = [], []
    for strip in (0, 1, n_strips - 1):
        first_key_row = int(np.clip(strip * NA_STRIP - NA_ROWS // 2, 0, GRID_ROWS - NA_KEY_ROWS))
        offset = first_key_row - strip * NA_STRIP + NA_ROWS - 1
        first_lane = lambda dr: (offset - dr + NA_STRIP) * GRID_W
        tables.append(jnp.stack(
            [by_query[:, :, first_lane(dr):first_lane(dr) + NA_N_LOC] for dr in range(NA_STRIP)],
            axis=1).reshape(NA_HEADS, TQ, NA_N_LOC))
        r = strip * NA_STRIP + np.arange(NA_STRIP)[:, None]
        r_start = np.clip(r - NA_ROWS // 2, 0, GRID_ROWS - NA_ROWS)
        key_row = first_key_row + np.arange(NA_KEY_ROWS)[None, :]
        in_rows = (key_row >= r_start) & (key_row < r_start + NA_ROWS)
        masks.append((in_rows[:, None, :, None] & in_cols[None, :, None, :]).reshape(TQ, NA_N_LOC))
    return jnp.where(np.stack(masks)[:, None], jnp.stack(tables), NEG_INF) * LOG2E


def _pad_heads(w, heads, width, keep):
    w = w.reshape(w.shape[0], heads, width)[:, :, keep[0]:keep[1]]
    return jnp.pad(w, ((0, 0), (0, 0), (0, LANES - (keep[1] - keep[0])))).reshape(w.shape[0], heads * LANES)


_WIN_HEAD_ORDER = (0, 2, 1, 3)


def _prep_w_in(w):
    seg = lambda a, b: w[:, a:b]
    kr = jnp.pad(seg(384, 416), ((0, 0), (MLA_NOPE, LANES - MLA_NOPE - MLA_ROPE)))
    wq = seg(1184, 1440).reshape(D_MODEL, WIN_HEADS, WIN_DIM)[:, np.array(_WIN_HEAD_ORDER)].reshape(D_MODEL, 256)
    return jnp.concatenate([seg(0, 384), kr, seg(416, 1184), wq, seg(1440, 2464)], axis=1).astype(BF16)


def _prep_w_branch(w):
    win = w[512:768].reshape(WIN_HEADS, WIN_DIM, D_MODEL)[np.array(_WIN_HEAD_ORDER)].reshape(256, D_MODEL)
    return jnp.concatenate([w[:512], win, w[768:]], axis=0).astype(BF16)


def _lambda_init(layer):
    return 0.8 - 0.6 * math.exp(-0.3 * layer)


def kernel(x, c, ctx, c_ctx, w_ada, b_ada, norm_mix, norm_ffn, w_in, mla_q_norm, mla_kv_norm, w_uq, w_ukv,
           diff_lambda, diff_subln, win_sink, na_rpb, w_gate, b_gate, w_branch, w_out, w_ff1, w_ff2, final_norm):
    cvec = jnp.concatenate([c, c_ctx[None, :], jnp.zeros((N_MOD_ROWS - BATCH - 1, D_MODEL), F32)], axis=0)
    mods = _modulation(cvec, w_ada, b_ada)
    rope_tab = _rope_tables()
    na_bias = jax.vmap(_na_bias_tables)(na_rpb)
    row = lambda v: v.reshape(1, -1)
    xc = ctx
    for l in range(DEPTH):
        need_ctx = l < DEPTH - 1
        lam_init = _lambda_init(l)
        ukv = w_ukv[l].reshape(MLA_KV_RANK, MLA_HEADS, MLA_NOPE + MLA_V)
        proj_w = (row(norm_mix[l]), _prep_w_in(w_in[l]), row(mla_q_norm[l]), row(mla_kv_norm[l]),
                  _pad_heads(w_uq[l], MLA_HEADS, MLA_NOPE + MLA_ROPE, (0, MLA_NOPE + MLA_ROPE)).astype(BF16),
                  _pad_heads(w_ukv[l], MLA_HEADS, MLA_NOPE + MLA_V, (0, MLA_NOPE)).astype(BF16),
                  ukv[:, :, MLA_NOPE:].reshape(MLA_KV_RANK, MLA_HEADS * MLA_V).astype(BF16))
        mq, mk, mv, dq, dk, dv, wq, wk, wv, nq, nk, nv = _project(x, mods[l], *proj_w, rope_tab, is_ctx=False)
        pc = _project(xc, mods[l], *proj_w, None, is_ctx=True)
        sub = jnp.tile(diff_subln[l], 2).reshape(1, LANES)
        branches = ((mq, mk, mv, pc[1], pc[2]), (dq, dk, dv, pc[4], pc[5]),
                    (wq, wk, wv, pc[7], pc[8]), (nq, nk, nv, pc[10], pc[11]))
        outs = _latent_attention(win_sink[l], branches, diff_lambda[l], sub, na_bias[l], lam_init)
        mix_w = (row(norm_mix[l]), w_gate[l].astype(BF16), row(b_gate[l]), _prep_w_branch(w_branch[l]),
                 w_out[l].astype(BF16), row(norm_ffn[l]), w_ff1[l].astype(BF16), w_ff2[l].astype(BF16),
                 row(final_norm))
        x = _channel_mix(x, mods[l], outs, *mix_w, is_ctx=False, final=not need_ctx)
        if need_ctx:
            outs_c = _ctx_attention(pc, win_sink[l], diff_lambda[l], sub, lam_init)
            xc = _channel_mix(xc, mods[l], outs_c, *mix_w, is_ctx=True, final=False)
    return x
```

```python
import functools
import math

import numpy as np
import jax
import jax.numpy as jnp
from jax import lax
from jax.experimental import pallas as pl
from jax.experimental.pallas import tpu as pltpu

D_MODEL = 1024
BATCH = 8
SEQ = 2048
DEPTH = 2
CTX_LEN = 256
GRID_W = 64
GRID_ROWS = SEQ // GRID_W
ROPE_BASE = 10000.0
NEG_INF = -1e30
EPS = 1e-6
N_ADA = 6
LOG2E = math.log2(math.e)

MLA_HEADS = 4
MLA_NOPE = 64
MLA_ROPE = 32
MLA_V = 64
MLA_Q_RANK = 256
MLA_KV_RANK = 128
MLA_SCALE = (MLA_NOPE + MLA_ROPE) ** -0.5
DIFF_HEADS = 4
DIFF_QK = 32
DIFF_V = 64
DIFF_SCALE = DIFF_QK ** -0.5
WIN_HEADS = 4
WIN_KV_HEADS = 2
WIN_DIM = 64
WINDOW = 128
WIN_SCALE = WIN_DIM ** -0.5
NA_HEADS = 4
NA_DIM = 64
NA_ROWS = 8
NA_COLS = 16
NA_SCALE = NA_DIM ** -0.5
FFN_HIDDEN = 4 * D_MODEL
N_BRANCH = 4
BRANCH_W = 256

LANES = 128
HALF = LANES // 2
VMEM_LIMIT = 56 * 1024 * 1024

BF16 = jnp.bfloat16
F32 = jnp.float32

P_CQ, P_CKV, P_KR = 0, 256, 384
P_DQ, P_DK, P_DV = 512, 768, 1024
P_WQ, P_WK, P_WV = 1280, 1536, 1664
P_NQ, P_NK, P_NV = 1792, 2048, 2304
P_WIDTH = 2560

TM_PROJ = 1024
TQ = 256
TQ_DENSE = 512
TM_LAT = 512
N_MOD_ROWS = 16
CTX_MOD_ROW = BATCH

NA_STRIP = TQ // GRID_W
NA_KEY_ROWS = 12
NA_N_LOC = NA_KEY_ROWS * GRID_W
NA_BIAS_VARIANTS = 3
WIN_N_LOC = TQ + 2 * WINDOW


def _const_spec(shape):
    nd = len(shape)
    return pl.BlockSpec(shape, lambda *_: (0,) * nd, pipeline_mode=pl.Buffered(1))


def _params(n_axes):
    return pltpu.CompilerParams(dimension_semantics=("arbitrary",) * n_axes, vmem_limit_bytes=VMEM_LIMIT)


def _rms(x, g):
    return x * lax.rsqrt(jnp.mean(x * x, axis=-1, keepdims=True) + EPS) * g


def _adaln(x, g, shift, scale):
    return _rms(x, g) * (1.0 + scale) + shift


def _dot(a, b):
    return jnp.dot(a, b, preferred_element_type=F32)


def _dot_nt(a, b):
    return lax.dot_general(a, b, (((1,), (1,)), ((), ())), preferred_element_type=F32)


def _lane_iota(rows):
    return lax.broadcasted_iota(jnp.int32, (rows, LANES), 1)


def _blk(i):
    return slice(LANES * i, LANES * (i + 1))


def _mod_kernel(c_ref, w_ref, b_ref, o_ref):
    c = c_ref[...]
    act = (c * jax.nn.sigmoid(c)).astype(BF16)
    o_ref[0] = _dot(act, w_ref[0].astype(BF16)) + b_ref[0]


def _modulation(cvec, w_ada, b_ada):
    tn = 3072
    n_ada = N_ADA * D_MODEL
    out = pl.pallas_call(
        _mod_kernel,
        grid=(DEPTH, n_ada // tn),
        in_specs=[
            pl.BlockSpec((N_MOD_ROWS, D_MODEL), lambda l, n: (0, 0)),
            pl.BlockSpec((1, D_MODEL, tn), lambda l, n: (l, 0, n)),
            pl.BlockSpec((1, 1, tn), lambda l, n: (l, 0, n)),
        ],
        out_specs=pl.BlockSpec((1, N_MOD_ROWS, tn), lambda l, n: (l, 0, n)),
        out_shape=jax.ShapeDtypeStruct((DEPTH, N_MOD_ROWS, n_ada), F32),
        compiler_params=_params(2),
        name="modulation",
    )(cvec, w_ada, b_ada.reshape(DEPTH, 1, n_ada))
    return out.reshape(DEPTH, N_MOD_ROWS, N_ADA, D_MODEL)


def _mod_spec(is_ctx):
    if is_ctx:
        return pl.BlockSpec((1, N_ADA, D_MODEL), lambda j, b: (CTX_MOD_ROW, 0, 0))
    return pl.BlockSpec((1, N_ADA, D_MODEL), lambda j, b: (b, 0, 0))


def _proj_kernel(x_ref, mod_ref, g_ref, win_ref, qn_ref, kvn_ref, wuq_ref, wuk_ref, wuv_ref, *rest, rotary):
    if rotary:
        rope_ref, rest = rest[0], rest[1:]
    mq_ref, mk_ref, mv_ref, dq_ref, dk_ref, dv_ref, wq_ref, wk_ref, wv_ref, nq_ref, nk_ref, nv_ref = rest
    mod = mod_ref[0]
    h = _adaln(x_ref[0], g_ref[...], mod[0:1], mod[1:2]).astype(BF16)
    lane = _lane_iota(h.shape[0])

    def seg(start, width):
        return _dot(h, win_ref[:, start:start + width])

    def rope(x, base, quarter):
        if not rotary:
            return x
        return (x * rope_ref[base]
                + pltpu.roll(x, LANES - quarter, 1) * rope_ref[base + 1]
                + pltpu.roll(x, quarter, 1) * rope_ref[base + 2])

    def put(ref, val, rope_base=None, quarter=None, scale=None):
        for i in range(val.shape[1] // LANES):
            blk = val[:, _blk(i)]
            if rope_base is not None:
                blk = rope(blk, rope_base, quarter)
            if scale is not None:
                blk = blk * scale
            ref[0, :, _blk(i)] = blk.astype(BF16)

    def put_values(ref, val):
        for hd in range(val.shape[1] // HALF):
            blk = val[:, _blk(hd // 2)]
            if hd % 2:
                blk = pltpu.roll(blk, HALF, 1)
            ref[0, :, _blk(hd)] = jnp.where(lane < HALF, blk, 1.0).astype(BF16)

    cq = _rms(seg(P_CQ, MLA_Q_RANK), qn_ref[...]).astype(BF16)
    put(mq_ref, _dot(cq, wuq_ref[...]), 0, MLA_ROPE // 4, MLA_SCALE * LOG2E)
    ckv = _rms(seg(P_CKV, MLA_KV_RANK), kvn_ref[...]).astype(BF16)
    kr = rope(seg(P_KR, LANES), 0, MLA_ROPE // 4)
    kn = _dot(ckv, wuk_ref[...])
    for i in range(MLA_HEADS):
        mk_ref[0, :, _blk(i)] = (kn[:, _blk(i)] + kr).astype(BF16)
    put_values(mv_ref, _dot(ckv, wuv_ref[...]))
    put(dq_ref, seg(P_DQ, 256), 3, DIFF_QK // 4, DIFF_SCALE * LOG2E)
    put(dk_ref, seg(P_DK, 256), 3, DIFF_QK // 4)
    put_values(dv_ref, seg(P_DV, 256))
    put(wq_ref, seg(P_WQ, 256), 6, WIN_DIM // 4, WIN_SCALE * LOG2E)
    put(wk_ref, seg(P_WK, 128), 6, WIN_DIM // 4)
    put_values(wv_ref, seg(P_WV, 128))
    put(nq_ref, seg(P_NQ, 256), scale=NA_SCALE * LOG2E)
    put(nk_ref, seg(P_NK, 256))
    put_values(nv_ref, seg(P_NV, 256))


_PROJ_OUT_WIDTHS = (512, 512, 512, 256, 256, 512, 256, 128, 256, 256, 256, 512)


def _project(xs, mods, g, w_in_p, q_norm, kv_norm, w_uq_p, w_uk_p, w_uv, rope_tab, is_ctx):
    n_tok = xs.shape[1]
    tm = min(TM_PROJ, n_tok)
    tok = lambda width: pl.BlockSpec((1, tm, width), lambda j, b: (b, j, 0))
    in_specs = [
        tok(D_MODEL),
        _mod_spec(is_ctx),
        _const_spec((1, D_MODEL)),
        _const_spec((D_MODEL, P_WIDTH)),
        _const_spec((1, MLA_Q_RANK)),
        _const_spec((1, MLA_KV_RANK)),
        _const_spec((MLA_Q_RANK, MLA_HEADS * LANES)),
        _const_spec((MLA_KV_RANK, MLA_HEADS * LANES)),
        _const_spec((MLA_KV_RANK, MLA_HEADS * MLA_V)),
    ]
    args = [xs, mods, g, w_in_p, q_norm, kv_norm, w_uq_p, w_uk_p, w_uv]
    if not is_ctx:
        in_specs.append(pl.BlockSpec((9, tm, LANES), lambda j, b: (0, j, 0)))
        args.append(rope_tab)
    return pl.pallas_call(
        functools.partial(_proj_kernel, rotary=not is_ctx),
        grid=(n_tok // tm, BATCH),
        in_specs=in_specs,
        out_specs=[tok(w) for w in _PROJ_OUT_WIDTHS],
        out_shape=[jax.ShapeDtypeStruct((BATCH, n_tok, w), BF16) for w in _PROJ_OUT_WIDTHS],
        compiler_params=_params(2),
        name="project_ctx" if is_ctx else "project",
    )(*args)


def _attend(q, ks, vs, lane, post=None, sink=None):
    ss = []
    for i, k in enumerate(ks):
        s = _dot_nt(q, k)
        if post is not None and post[i] is not None:
            s = post[i](s)
        ss.append(s)
    m = ss[0].max(axis=-1, keepdims=True)
    for s in ss[1:]:
        m = jnp.maximum(m, s.max(axis=-1, keepdims=True))
    if sink is not None:
        m = jnp.maximum(m, sink)
    acc = None
    for s, v in zip(ss, vs):
        part = _dot(jnp.exp2(s - m).astype(BF16), v)
        acc = part if acc is None else acc + part
    if sink is not None:
        acc = acc + jnp.where(lane >= HALF, jnp.exp2(sink - m), 0.0)
    return acc


def _normalise(acc):
    return acc * (1.0 / pltpu.roll(acc, HALF, 1))


def _pair(lane, even, odd):
    return jnp.where(lane < HALF, even, pltpu.roll(odd, HALF, 1))


def _half_mask(lane, q, start, width):
    return jnp.where((lane >= start) & (lane < start + width), q, jnp.zeros_like(q))


def _mla_heads(q_ref, kv_parts, o_ref, lane):
    for blk in range(MLA_HEADS // 2):
        outs = []
        for hd in (2 * blk, 2 * blk + 1):
            acc = _attend(q_ref[0, :, _blk(hd)], [k[0, :, _blk(hd)] for k, _ in kv_parts],
                          [v[0, :, _blk(hd)] for _, v in kv_parts], lane)
            outs.append(_normalise(acc))
        o_ref[0, :, _blk(blk)] = _pair(lane, *outs).astype(BF16)


def _diff_lambda(lam_ref, lam_init):
    lp = lam_ref[...]
    return (jnp.exp(jnp.sum(lp[0:1] * lp[1:2], axis=-1, keepdims=True))
            - jnp.exp(jnp.sum(lp[2:3] * lp[3:4], axis=-1, keepdims=True)) + lam_init)


def _diff_heads(q_ref, kv_parts, lam_ref, sub_ref, o_ref, lane, lam_init):
    lam = _diff_lambda(lam_ref, lam_init)
    for blk in range(DIFF_HEADS // 2):
        qb = q_ref[0, :, _blk(blk)]
        ks = [k[0, :, _blk(blk)] for k, _ in kv_parts]
        outs = []
        for u in range(2):
            vs = [v[0, :, _blk(2 * blk + u)] for _, v in kv_parts]
            a1 = _attend(_half_mask(lane, qb, HALF * u, DIFF_QK), ks, vs, lane)
            a2 = _attend(_half_mask(lane, qb, HALF * u + DIFF_QK, DIFF_QK), ks, vs, lane)
            outs.append(_normalise(a1) - lam * _normalise(a2))
        pair = _pair(lane, *outs)
        sq = pair * pair
        lo = jnp.sum(jnp.where(lane < HALF, sq, 0.0), axis=-1, keepdims=True)
        hi = jnp.sum(jnp.where(lane < HALF, 0.0, sq), axis=-1, keepdims=True)
        ms = jnp.where(lane < HALF, lo, hi) * (1.0 / DIFF_V)
        o_ref[0, :, _blk(blk)] = (pair * lax.rsqrt(ms + EPS) * sub_ref[...] * (1.0 - lam_init)).astype(BF16)


def _win_heads(q_ref, k_parts, v_parts, posts, sink_ref, o_ref, lane):
    for blk in range(2):
        qb = q_ref[0, :, _blk(blk)]
        outs = []
        for u in range(WIN_KV_HEADS):
            acc = _attend(_half_mask(lane, qb, HALF * u, HALF), k_parts, [v[:, _blk(u)] for v in v_parts], lane,
                          post=posts, sink=sink_ref[blk + 2 * u] * LOG2E)
            outs.append(_normalise(acc))
        o_ref[0, :, _blk(blk)] = _pair(lane, *outs).astype(BF16)


def _na_heads(q_ref, kv_parts, posts_of_head, o_ref, lane):
    for blk in range(NA_HEADS // 2):
        qb = q_ref[0, :, _blk(blk)]
        outs = []
        for u in range(2):
            hd = 2 * blk + u
            acc = _attend(_half_mask(lane, qb, HALF * u, HALF), [k(_blk(blk)) for k, _ in kv_parts],
                          [v(_blk(hd)) for _, v in kv_parts], lane, post=posts_of_head(hd))
            outs.append(_normalise(acc))
        o_ref[0, :, _blk(blk)] = _pair(lane, *outs).astype(BF16)


def _win_tile(sink_ref, q_ref, kl_ref, vl_ref, kc_ref, vc_ref, o_ref, j):
    start = pl.multiple_of(jnp.clip(j * TQ - WINDOW, 0, SEQ - WIN_N_LOC), WINDOW)
    qpos = j * TQ + lax.broadcasted_iota(jnp.int32, (TQ, WIN_N_LOC), 0)
    kpos = start + lax.broadcasted_iota(jnp.int32, (TQ, WIN_N_LOC), 1)
    allowed = jnp.abs(qpos - kpos) <= WINDOW
    band = lambda s: jnp.where(allowed, s, NEG_INF)
    _win_heads(q_ref, [kl_ref[0, pl.ds(start, WIN_N_LOC), :], kc_ref[0]],
               [vl_ref[0, pl.ds(start, WIN_N_LOC), :], vc_ref[0]], [band, None], sink_ref, o_ref, _lane_iota(TQ))


def _na_tile(q_ref, kl_ref, vl_ref, kc_ref, vc_ref, bias_ref, o_ref, j):
    n_strips = GRID_ROWS // NA_STRIP
    first = jnp.clip(j * NA_STRIP - NA_ROWS // 2, 0, GRID_ROWS - NA_KEY_ROWS)
    start = pl.multiple_of(first * GRID_W, GRID_W)
    variant = (j >= 1).astype(jnp.int32) + (j >= n_strips - 1).astype(jnp.int32)
    parts = [(lambda ls: kl_ref[0, pl.ds(start, NA_N_LOC), ls], lambda ls: vl_ref[0, pl.ds(start, NA_N_LOC), ls]),
             (lambda ls: kc_ref[0, :, ls], lambda ls: vc_ref[0, :, ls])]
    posts = lambda hd: [lambda s: s + bias_ref[variant, hd], None]
    _na_heads(q_ref, parts, posts, o_ref, _lane_iota(TQ))


def _attention_kernel(sink_ref, mq, mkl, mvl, mkc, mvc, dq, dkl, dvl, dkc, dvc, wq, wkl, wvl, wkc, wvc,
                      nq, nkl, nvl, nkc, nvc, lam_ref, sub_ref, bias_ref, oa_ref, ob_ref, oc_ref, od_ref,
                      *, lam_init):
    lane = _lane_iota(TQ_DENSE)
    _mla_heads(mq, [(mkl, mvl), (mkc, mvc)], oa_ref, lane)
    _diff_heads(dq, [(dkl, dvl), (dkc, dvc)], lam_ref, sub_ref, ob_ref, lane, lam_init)
    for t in range(TQ_DENSE // TQ):
        rows = pl.ds(t * TQ, TQ)
        j = pl.program_id(1) * (TQ_DENSE // TQ) + t
        _win_tile(sink_ref, wq.at[:, rows], wkl, wvl, wkc, wvc, oc_ref.at[:, rows], j)
        _na_tile(nq.at[:, rows], nkl, nvl, nkc, nvc, bias_ref, od_ref.at[:, rows], j)


def _qkv_specs(q, kl, vl):
    qw, kw, vw = q.shape[-1], kl.shape[-1], vl.shape[-1]
    return [pl.BlockSpec((1, TQ_DENSE, qw), lambda b, j: (b, j, 0)),
            pl.BlockSpec((1, SEQ, kw), lambda b, j: (b, 0, 0)),
            pl.BlockSpec((1, SEQ, vw), lambda b, j: (b, 0, 0)),
            pl.BlockSpec((1, CTX_LEN, kw), lambda b, j: (b, 0, 0)),
            pl.BlockSpec((1, CTX_LEN, vw), lambda b, j: (b, 0, 0))]


def _latent_attention(sink, branches, lam, sub, bias, lam_init):
    out_spec = pl.BlockSpec((1, TQ_DENSE, BRANCH_W), lambda b, j: (b, j, 0))
    return pl.pallas_call(
        functools.partial(_attention_kernel, lam_init=lam_init),
        grid=(BATCH, SEQ // TQ_DENSE),
        in_specs=[pl.BlockSpec(memory_space=pltpu.SMEM)] + [s for br in branches for s in _qkv_specs(*br[:3])]
        + [_const_spec((4, DIFF_QK)), _const_spec((1, LANES)), _const_spec(bias.shape)],
        out_specs=[out_spec] * N_BRANCH,
        out_shape=[jax.ShapeDtypeStruct((BATCH, SEQ, BRANCH_W), BF16)] * N_BRANCH,
        compiler_params=_params(2),
        name="latent_attention",
    )(sink, *[a for br in branches for a in br], lam, sub, bias)


def _ctx_kernel(sink_ref, mq, mk, mv, dq, dk, dv, wq, wk, wv, nq, nk, nv, lam_ref, sub_ref,
                oa_ref, ob_ref, oc_ref, od_ref, *, lam_init):
    lane = _lane_iota(CTX_LEN)
    _mla_heads(mq, [(mk, mv)], oa_ref, lane)
    _diff_heads(dq, [(dk, dv)], lam_ref, sub_ref, ob_ref, lane, lam_init)
    _win_heads(wq, [wk[0]], [wv[0]], None, sink_ref, oc_ref, lane)
    _na_heads(nq, [(lambda ls: nk[0, :, ls], lambda ls: nv[0, :, ls])], lambda hd: None, od_ref, lane)


def _ctx_attention(proj_ctx, sink, lam, sub, lam_init):
    tok = lambda width: pl.BlockSpec((1, CTX_LEN, width), lambda b: (b, 0, 0))
    return pl.pallas_call(
        functools.partial(_ctx_kernel, lam_init=lam_init),
        grid=(BATCH,),
        in_specs=[pl.BlockSpec(memory_space=pltpu.SMEM)] + [tok(w) for w in _PROJ_OUT_WIDTHS]
        + [_const_spec((4, DIFF_QK)), _const_spec((1, LANES))],
        out_specs=[tok(BRANCH_W)] * N_BRANCH,
        out_shape=[jax.ShapeDtypeStruct((BATCH, CTX_LEN, BRANCH_W), BF16)] * N_BRANCH,
        compiler_params=_params(1),
        name="ctx_attention",
    )(sink, *proj_ctx, lam, sub)


def _channel_kernel(x_ref, mod_ref, gm_ref, oa_ref, ob_ref, oc_ref, od_ref, wg_ref, bg_ref, wb_ref, wo_ref,
                    gf_ref, w1_ref, w2_ref, fn_ref, out_ref, *, final):
    x = x_ref[0]
    mod = mod_ref[0]
    h = _adaln(x, gm_ref[...], mod[0:1], mod[1:2]).astype(BF16)
    y = None
    for i, o_ref in enumerate((oa_ref, ob_ref, oc_ref, od_ref)):
        cols = slice(D_MODEL * i, D_MODEL * (i + 1))
        gate = jax.nn.sigmoid(_dot(h, wg_ref[:, cols]) + bg_ref[:, cols])
        term = gate * _dot(o_ref[0], wb_ref[BRANCH_W * i:BRANCH_W * (i + 1), :])
        y = term if y is None else y + term
    x = x + mod[2:3] * _dot(y.astype(BF16), wo_ref[...])
    h = _adaln(x, gf_ref[...], mod[3:4], mod[4:5]).astype(BF16)
    acc = None
    for i in range(FFN_HIDDEN // D_MODEL):
        cols = slice(D_MODEL * i, D_MODEL * (i + 1))
        u = jnp.square(jnp.maximum(_dot(h, w1_ref[:, cols]), 0.0)).astype(BF16)
        part = _dot(u, w2_ref[cols, :])
        acc = part if acc is None else acc + part
    out = x + mod[5:6] * acc
    out_ref[0] = _rms(out, fn_ref[...]) if final else out


def _channel_mix(xs, mods, outs, g_mix, w_gate, b_gate, w_branch, w_out, g_ffn, w1, w2, final_norm, is_ctx, final):
    tm = CTX_LEN if is_ctx else TM_LAT
    tok = lambda width: pl.BlockSpec((1, tm, width), lambda j, b: (b, j, 0))
    return pl.pallas_call(
        functools.partial(_channel_kernel, final=final),
        grid=(xs.shape[1] // tm, BATCH),
        in_specs=[tok(D_MODEL), _mod_spec(is_ctx), _const_spec((1, D_MODEL))] + [tok(BRANCH_W)] * N_BRANCH + [
            _const_spec((D_MODEL, N_BRANCH * D_MODEL)),
            _const_spec((1, N_BRANCH * D_MODEL)),
            _const_spec((N_BRANCH * BRANCH_W, D_MODEL)),
            _const_spec((D_MODEL, D_MODEL)),
            _const_spec((1, D_MODEL)),
            _const_spec((D_MODEL, FFN_HIDDEN)),
            _const_spec((FFN_HIDDEN, D_MODEL)),
            _const_spec((1, D_MODEL)),
        ],
        out_specs=tok(D_MODEL),
        out_shape=jax.ShapeDtypeStruct(xs.shape, F32),
        compiler_params=_params(2),
        name="channel_mix_ctx" if is_ctx else ("channel_mix_final" if final else "channel_mix"),
    )(xs, mods, g_mix, *outs, w_gate, b_gate, w_branch, w_out, g_ffn, w1, w2, final_norm)


def _rope_tables():
    t = jnp.arange(SEQ, dtype=jnp.int32)
    row = (t // GRID_W).astype(F32)[:, None]
    col = (t % GRID_W).astype(F32)[:, None]

    def table(dim, lane_start, lane_stop):
        quarter = dim // 4
        inv_freq = ROPE_BASE ** (-jnp.arange(quarter, dtype=F32) / quarter)
        ang = jnp.concatenate([row * inv_freq, row * inv_freq, col * inv_freq, col * inv_freq], axis=-1)
        reps = (lane_stop - lane_start) // dim
        cos = jnp.tile(jnp.cos(ang), (1, reps))
        sin = jnp.tile(jnp.sin(ang), (1, reps))
        lower = (np.arange(lane_stop - lane_start) % (2 * quarter)) < quarter
        pad = lambda a, fill: jnp.pad(a, ((0, 0), (lane_start, LANES - lane_stop)), constant_values=fill)
        return [pad(cos, 1.0), pad(jnp.where(lower, -sin, 0.0), 0.0), pad(jnp.where(lower, 0.0, sin), 0.0)]

    tabs = table(MLA_ROPE, MLA_NOPE, MLA_NOPE + MLA_ROPE) + table(DIFF_QK, 0, LANES) + table(WIN_DIM, 0, LANES)
    return jnp.stack(tabs)


def _na_bias_tables(rpb):
    n_dr = 2 * NA_ROWS - 1
    period = 2 * GRID_W - 1
    pad = GRID_W - NA_COLS
    rp = jnp.pad(rpb.astype(F32), ((0, 0), (0, 0), (pad, pad)))
    flat = jnp.tile(rp, (1, 1, GRID_W + 1))[:, :, :GRID_W * (period + 1)]
    by_col = flat.reshape(NA_HEADS, n_dr, GRID_W, period + 1)[:, :, ::-1, :GRID_W]
    by_query = jnp.pad(jnp.transpose(by_col, (0, 2, 1, 3)), ((0, 0), (0, 0), (NA_STRIP, NA_STRIP), (0, 0)))
    by_query = by_query.reshape(NA_HEADS, GRID_W, (n_dr + 2 * NA_STRIP) * GRID_W)
    qc = np.arange(GRID_W)[:, None]
    kc = np.arange(GRID_W)[None, :]
    c_start = np.clip(qc - NA_COLS // 2, 0, GRID_W - NA_COLS)
    in_cols = (kc >= c_start) & (kc < c_start + NA_COLS)
    n_strips = GRID_ROWS // NA_STRIP
    tables, masks = [], []
    for strip in (0, 1, n_strips - 1):
        first_key_row = int(np.clip(strip * NA_STRIP - NA_ROWS // 2, 0, GRID_ROWS - NA_KEY_ROWS))
        offset = first_key_row - strip * NA_STRIP + NA_ROWS - 1
        first_lane = lambda dr: (offset - dr + NA_STRIP) * GRID_W
        tables.append(jnp.stack(
            [by_query[:, :, first_lane(dr):first_lane(dr) + NA_N_LOC] for dr in range(NA_STRIP)],
            axis=1).reshape(NA_HEADS, TQ, NA_N_LOC))
        r = strip * NA_STRIP + np.arange(NA_STRIP)[:, None]
        r_start = np.clip(r - NA_ROWS // 2, 0, GRID_ROWS - NA_ROWS)
        key_row = first_key_row + np.arange(NA_KEY_ROWS)[None, :]
        in_rows = (key_row >= r_start) & (key_row < r_start + NA_ROWS)
        masks.append((in_rows[:, None, :, None] & in_cols[None, :, None, :]).reshape(TQ, NA_N_LOC))
    return jnp.where(np.stack(masks)[:, None], jnp.stack(tables), NEG_INF) * LOG2E


def _pad_heads(w, heads, width, keep):
    w = w.reshape(w.shape[0], heads, width)[:, :, keep[0]:keep[1]]
    return jnp.pad(w, ((0, 0), (0, 0), (0, LANES - (keep[1] - keep[0])))).reshape(w.shape[0], heads * LANES)


_WIN_HEAD_ORDER = (0, 2, 1, 3)


def _prep_w_in(w):
    seg = lambda a, b: w[:, a:b]
    kr = jnp.pad(seg(384, 416), ((0, 0), (MLA_NOPE, LANES - MLA_NOPE - MLA_ROPE)))
    wq = seg(1184, 1440).reshape(D_MODEL, WIN_HEADS, WIN_DIM)[:, np.array(_WIN_HEAD_ORDER)].reshape(D_MODEL, 256)
    return jnp.concatenate([seg(0, 384), kr, seg(416, 1184), wq, seg(1440, 2464)], axis=1).astype(BF16)


def _prep_w_branch(w):
    win = w[512:768].reshape(WIN_HEADS, WIN_DIM, D_MODEL)[np.array(_WIN_HEAD_ORDER)].reshape(256, D_MODEL)
    return jnp.concatenate([w[:512], win, w[768:]], axis=0).astype(BF16)


def _lambda_init(layer):
    return 0.8 - 0.6 * math.exp(-0.3 * layer)


def kernel(x, c, ctx, c_ctx, w_ada, b_ada, norm_mix, norm_ffn, w_in, mla_q_norm, mla_kv_norm, w_uq, w_ukv,
           diff_lambda, diff_subln, win_sink, na_rpb, w_gate, b_gate, w_branch, w_out, w_ff1, w_ff2, final_norm):
    cvec = jnp.concatenate([c, c_ctx[None, :], jnp.zeros((N_MOD_ROWS - BATCH - 1, D_MODEL), F32)], axis=0)
    mods = _modulation(cvec, w_ada, b_ada)
    rope_tab = _rope_tables()
    na_bias = jax.vmap(_na_bias_tables)(na_rpb)
    row = lambda v: v.reshape(1, -1)
    xc = ctx
    for l in range(DEPTH):
        need_ctx = l < DEPTH - 1
        lam_init = _lambda_init(l)
        ukv = w_ukv[l].reshape(MLA_KV_RANK, MLA_HEADS, MLA_NOPE + MLA_V)
        proj_w = (row(norm_mix[l]), _prep_w_in(w_in[l]), row(mla_q_norm[l]), row(mla_kv_norm[l]),
                  _pad_heads(w_uq[l], MLA_HEADS, MLA_NOPE + MLA_ROPE, (0, MLA_NOPE + MLA_ROPE)).astype(BF16),
                  _pad_heads(w_ukv[l], MLA_HEADS, MLA_NOPE + MLA_V, (0, MLA_NOPE)).astype(BF16),
                  ukv[:, :, MLA_NOPE:].reshape(MLA_KV_RANK, MLA_HEADS * MLA_V).astype(BF16))
        mq, mk, mv, dq, dk, dv, wq, wk, wv, nq, nk, nv = _project(x, mods[l], *proj_w, rope_tab, is_ctx=False)
        pc = _project(xc, mods[l], *proj_w, None, is_ctx=True)
        sub = jnp.tile(diff_subln[l], 2).reshape(1, LANES)
        branches = ((mq, mk, mv, pc[1], pc[2]), (dq, dk, dv, pc[4], pc[5]),
                    (wq, wk, wv, pc[7], pc[8]), (nq, nk, nv, pc[10], pc[11]))
        outs = _latent_attention(win_sink[l], branches, diff_lambda[l], sub, na_bias[l], lam_init)
        mix_w = (row(norm_mix[l]), w_gate[l].astype(BF16), row(b_gate[l]), _prep_w_branch(w_branch[l]),
                 w_out[l].astype(BF16), row(norm_ffn[l]), w_ff1[l].astype(BF16), w_ff2[l].astype(BF16),
                 row(final_norm))
        x = _channel_mix(x, mods[l], outs, *mix_w, is_ctx=False, final=not need_ctx)
        if need_ctx:
            outs_c = _ctx_attention(pc, win_sink[l], diff_lambda[l], sub, lam_init)
            xc = _channel_mix(xc, mods[l], outs_c, *mix_w, is_ctx=True, final=False)
    return x
```
